```python
import math
import jax, jax.numpy as jnp
from jax import lax
import numpy as np

D_MODEL = 2048
BATCH = 2
SEQ = 4096
DEPTH = 2
DEC_BATCH = 32
DEC_SEQ = 1
PAST_LEN = 8192
PAGE_SIZE = 128

N_HEADS = 16
HEAD_DIM = 128
KV_HEADS = 4
GROUP = N_HEADS // KV_HEADS
CMP_LEN = 32
CMP_STRIDE = 16
CMP_HIDDEN = 128
SEL_LEN = 64
N_SEL = 16
WINDOW = 512
N_NSA_BRANCH = 3
FORCE = 1e4
Q_CHUNK = 64
ROPE_THETA = 10000.0
POOL_WINDOWS = (2, 4, 8, 16)
D_POOL = D_MODEL // 2
POOL_GROUP = D_POOL // len(POOL_WINDOWS)
POOL_KEEP = max(POOL_WINDOWS) - 1
N_BRANCH = 2
D_FF = 5632
N_EXPERTS = 8
TOP_K = 2
D_FF_EXPERT = 7168
EPS = 1e-6
NEG = -1e30
SCALE = HEAD_DIM ** -0.5
D_ATTN = N_HEADS * HEAD_DIM
D_KV = KV_HEADS * HEAD_DIM
N_KV_ROWS = 4
N_WIN_ROWS = 2
D_IN = D_POOL + D_ATTN + 6 * D_KV + N_HEADS * N_NSA_BRANCH + N_BRANCH * D_MODEL
N_DENSE_LAYERS = (DEPTH + 1) // 2
N_MOE_LAYERS = DEPTH // 2

kernel_name = 'pool_nsa_gated_hybrid_decode_step'

f32 = jnp.float32


def rms_norm(x, g):
    xf = x.astype(f32)
    y = xf * lax.rsqrt(jnp.mean(xf * xf, axis=-1, keepdims=True) + EPS)
    return (y * g.astype(f32)).astype(x.dtype)


def rope(x, pos):
    half = HEAD_DIM // 2
    inv = jnp.power(ROPE_THETA, -jnp.arange(0, HEAD_DIM, 2, dtype=f32) / HEAD_DIM)
    ang = pos.astype(f32)[:, None] * inv[None, :]
    cos = jnp.cos(ang)[None, :, None, :]
    sin = jnp.sin(ang)[None, :, None, :]
    xf = x.astype(f32)
    x1, x2 = xf[..., :half], xf[..., half:]
    return jnp.concatenate([x1 * cos - x2 * sin, x2 * cos + x1 * sin], axis=-1).astype(x.dtype)


def masked_softmax(s, mask):
    s = jnp.where(mask, s, NEG)
    p = jax.nn.softmax(s, axis=-1)
    return jnp.where(jnp.any(mask, axis=-1, keepdims=True), p, 0.0)


def pool_mix(u, prefix, t0, w_grp, scale):
    B, T, _ = u.shape
    ext = jnp.concatenate([prefix, u], axis=1).astype(f32)
    cs = jnp.pad(jnp.cumsum(ext, axis=1), ((0, 0), (1, 0), (0, 0)))
    pos = t0 + jnp.arange(T)
    uf = u.astype(f32)
    groups = []
    for g, w in enumerate(POOL_WINDOWS):
        c = slice(g * POOL_GROUP, (g + 1) * POOL_GROUP)
        win_sum = cs[:, POOL_KEEP + 1:POOL_KEEP + 1 + T, c] - cs[:, POOL_KEEP + 1 - w:POOL_KEEP + 1 - w + T, c]
        cnt = jnp.minimum(w, pos + 1).astype(f32)[None, :, None]
        groups.append(win_sum / cnt - uf[:, :, c])
    pooled = jnp.stack(groups, axis=2).astype(u.dtype)
    mixed = jnp.einsum('btgc,gce->btge', pooled, w_grp).reshape(B, T, D_POOL)
    return mixed * scale


def compress(k, w1, b1, w2):
    B, Tp = k.shape[0], k.shape[1]
    n_sub = Tp // CMP_STRIDE
    r = CMP_LEN // CMP_STRIDE
    n_cmp = n_sub - r + 1
    sub = k.reshape(B, n_sub, CMP_STRIDE, KV_HEADS, HEAD_DIM)
    w1r = w1.reshape(r, CMP_STRIDE, HEAD_DIM, CMP_HIDDEN)
    part = jnp.einsum('bmlkd,rldh->rbmkh', sub, w1r)
    hid = part[0, :, :n_cmp]
    for j in range(1, r):
        hid = hid + part[j, :, j:j + n_cmp]
    hid = jax.nn.gelu(hid + b1)
    return jnp.einsum('bnkh,he->bnke', hid, w2)


def cmp_to_sel(n_cmp, n_blk):
    i0 = jnp.arange(n_cmp)[:, None] * CMP_STRIDE
    j0 = jnp.arange(n_blk)[None, :] * SEL_LEN
    ov = jnp.clip(jnp.minimum(i0 + CMP_LEN, j0 + SEL_LEN) - jnp.maximum(i0, j0), 0, None)
    return ov.astype(f32) / CMP_LEN


def nsa_attend(q, gate, k_cmp, v_cmp, k_sel, v_sel, k_win, v_win, wpos0, cw1, cb1, cw2):
    B, Tq = q.shape[0], q.shape[1]
    Tk = k_cmp.shape[1]
    q0 = Tk - Tq
    Tp = -(-Tk // SEL_LEN) * SEL_LEN
    pad = ((0, 0), (0, Tp - Tk), (0, 0), (0, 0))
    kc = compress(jnp.pad(k_cmp, pad), cw1[0], cb1[0], cw2[0])
    vc = compress(jnp.pad(v_cmp, pad), cw1[1], cb1[1], cw2[1])
    n_cmp = kc.shape[1]
    n_blk = Tp // SEL_LEN
    n_sel = min(N_SEL, n_blk)
    ks_blk = jnp.pad(k_sel, pad).reshape(B, n_blk, SEL_LEN, KV_HEADS, HEAD_DIM).transpose(0, 3, 1, 2, 4)
    vs_blk = jnp.pad(v_sel, pad).reshape(B, n_blk, SEL_LEN, KV_HEADS, HEAD_DIM).transpose(0, 3, 1, 2, 4)
    kw = jnp.pad(k_win, ((0, 0), (WINDOW, 0), (0, 0), (0, 0)))
    vw = jnp.pad(v_win, ((0, 0), (WINDOW, 0), (0, 0), (0, 0)))
    cmp_end = jnp.arange(n_cmp) * CMP_STRIDE + (CMP_LEN - 1)
    ov = cmp_to_sel(n_cmp, n_blk)
    blk_ids = jnp.arange(n_blk)
    qc = math.gcd(Q_CHUNK, Tq)
    n_ch = Tq // qc
    q_ch = jnp.moveaxis(q.reshape(B, n_ch, qc, KV_HEADS, GROUP, HEAD_DIM), 1, 0)
    g_ch = jnp.moveaxis(jax.nn.sigmoid(gate.astype(f32)).reshape(B, n_ch, qc, KV_HEADS, GROUP, N_NSA_BRANCH), 1, 0)
    gather = jax.vmap(jax.vmap(lambda blk, ix: blk[ix]))

    def one_chunk(args):
        c, qb, gb = args
        t = q0 + c * qc + jnp.arange(qc)
        s = jnp.einsum('bqkgd,bnkd->bkgqn', qb, kc).astype(f32) * SCALE
        p_c = masked_softmax(s, cmp_end[None, :] <= t[:, None])
        o_c = jnp.einsum('bkgqn,bnkd->bqkgd', p_c.astype(vc.dtype), vc)
        imp = jnp.einsum('bkgqn,nj->bkqj', p_c, ov)
        cur = (t // SEL_LEN)[:, None]
        valid = blk_ids[None, :] * SEL_LEN <= t[:, None]
        forced = (blk_ids[None, :] == 0) | (blk_ids[None, :] == cur) | (blk_ids[None, :] == cur - 1)
        imp = jnp.where(valid, jnp.where(forced, FORCE, imp), -jnp.inf)
        _, idx = lax.top_k(imp, n_sel)
        kg = gather(ks_blk, idx).reshape(B, KV_HEADS, qc, n_sel * SEL_LEN, HEAD_DIM)
        vg = gather(vs_blk, idx).reshape(B, KV_HEADS, qc, n_sel * SEL_LEN, HEAD_DIM)
        tok = (idx[..., None] * SEL_LEN + jnp.arange(SEL_LEN)).reshape(B, KV_HEADS, qc, n_sel * SEL_LEN)
        s = jnp.einsum('bqkgd,bkqsd->bkgqs', qb, kg).astype(f32) * SCALE
        p_s = masked_softmax(s, (tok <= t[None, None, :, None])[:, :, None])
        o_s = jnp.einsum('bkgqs,bkqsd->bqkgd', p_s.astype(vg.dtype), vg)
        start = q0 - wpos0 + c * qc
        kwb = lax.dynamic_slice_in_dim(kw, start, WINDOW + qc, axis=1)
        vwb = lax.dynamic_slice_in_dim(vw, start, WINDOW + qc, axis=1)
        kpos = wpos0 - WINDOW + start + jnp.arange(WINDOW + qc)
        mw = (kpos[None, :] >= wpos0) & (kpos[None, :] <= t[:, None]) & (kpos[None, :] > t[:, None] - WINDOW)
        s = jnp.einsum('bqkgd,bskd->bkgqs', qb, kwb).astype(f32) * SCALE
        p_w = masked_softmax(s, mw)
        o_w = jnp.einsum('bkgqs,bskd->bqkgd', p_w.astype(vwb.dtype), vwb)
        o = gb[..., 0:1] * o_c + gb[..., 1:2] * o_s + gb[..., 2:3] * o_w
        return o.astype(q.dtype)

    out = lax.map(one_chunk, (jnp.arange(n_ch), q_ch, g_ch))
    return jnp.moveaxis(out, 0, 1).reshape(B, Tq, D_ATTN)


def swiglu(x, wg, wu, wd):
    return (jax.nn.silu(x @ wg) * (x @ wu)) @ wd


def moe(x, w_router, wg, wu, wd):
    logits = (x @ w_router).astype(f32)
    top_v, top_i = lax.top_k(logits, TOP_K)
    gates = jax.nn.softmax(top_v, axis=-1)
    comb = jnp.sum(jax.nn.one_hot(top_i, N_EXPERTS, dtype=f32) * gates[..., None], axis=-2).astype(x.dtype)
    out = jnp.zeros_like(x)
    for e in range(N_EXPERTS):
        out = out + comb[..., e:e + 1] * swiglu(x, wg[e], wu[e], wd[e])
    return out


def trunk(x, t0, kv_cache, page_table, win_state, pool_state,
          norm_mix, w_in, pool_w, pool_scale, cmp_w1, cmp_b1, cmp_w2,
          w_pool_proj, w_attn_proj, w_out, norm_ffn, ffn_wg, ffn_wu, ffn_wd,
          moe_router, moe_wg, moe_wu, moe_wd, norm_final):
    B, T, _ = x.shape
    pos = t0 + jnp.arange(T)
    kv_new, win_new, pool_new = [], [], []
    for l in range(DEPTH):
        xn = rms_norm(x, norm_mix[l])
        proj = xn @ w_in[l]
        o = 0
        u = proj[..., o:o + D_POOL]
        o += D_POOL
        q = proj[..., o:o + D_ATTN].reshape(B, T, N_HEADS, HEAD_DIM)
        o += D_ATTN
        kv = proj[..., o:o + 6 * D_KV].reshape(B, T, 6, KV_HEADS, HEAD_DIM)
        o += 6 * D_KV
        g_nsa = proj[..., o:o + N_HEADS * N_NSA_BRANCH].reshape(B, T, KV_HEADS, GROUP, N_NSA_BRANCH)
        o += N_HEADS * N_NSA_BRANCH
        g_mrg = proj[..., o:].reshape(B, T, N_BRANCH, D_MODEL)
        q = rope(q, pos).reshape(B, T, KV_HEADS, GROUP, HEAD_DIM)
        keys = rope(kv[:, :, 0::2].reshape(B, T, 3 * KV_HEADS, HEAD_DIM), pos).reshape(B, T, 3, KV_HEADS, HEAD_DIM)
        vals = kv[:, :, 1::2]
        rows = jnp.stack([keys[:, :, 0], vals[:, :, 0], keys[:, :, 1], vals[:, :, 1]], axis=2)
        wrows = jnp.stack([keys[:, :, 2], vals[:, :, 2]], axis=2)
        if kv_cache is None:
            kv_full, win_full, wpos0 = rows, wrows, 0
            pool_prefix = jnp.zeros((B, POOL_KEEP, D_POOL), u.dtype)
        else:
            n_pages = page_table.shape[1]
            past = kv_cache[l][page_table].reshape(B, n_pages * PAGE_SIZE, N_KV_ROWS, KV_HEADS, HEAD_DIM)
            kv_full = jnp.concatenate([past, rows], axis=1)
            win_full = jnp.concatenate([win_state[l], wrows], axis=1)
            wpos0 = t0 - win_state.shape[2]
            pool_prefix = pool_state[l]
        pool_out = pool_mix(u, pool_prefix, t0, pool_w[l], pool_scale[l])
        attn_out = nsa_attend(q, g_nsa, kv_full[:, :, 0], kv_full[:, :, 1], kv_full[:, :, 2], kv_full[:, :, 3],
                              win_full[:, :, 0], win_full[:, :, 1], wpos0, cmp_w1[l], cmp_b1[l], cmp_w2[l])
        gm = jax.nn.sigmoid(g_mrg.astype(f32)).astype(x.dtype)
        h = gm[:, :, 0] * (pool_out @ w_pool_proj[l]) + gm[:, :, 1] * (attn_out @ w_attn_proj[l])
        x = x + h @ w_out[l]
        xn = rms_norm(x, norm_ffn[l])
        if l % 2 == 0:
            x = x + swiglu(xn, ffn_wg[l // 2], ffn_wu[l // 2], ffn_wd[l // 2])
        else:
            x = x + moe(xn, moe_router[l // 2], moe_wg[l // 2], moe_wu[l // 2], moe_wd[l // 2])
        kv_new.append(rows)
        win_new.append(win_full[:, -min(WINDOW, win_full.shape[1]):])
        pool_new.append(jnp.concatenate([pool_prefix, u], axis=1)[:, -POOL_KEEP:])
    y = rms_norm(x, norm_final)
    return y, jnp.stack(kv_new, 0), jnp.stack(win_new, 0), jnp.stack(pool_new, 0)


def setup_inputs(seed: int = 0) -> dict:
    key = jax.random.key(seed)
    ks = jax.random.split(key, 32)

    def nrm(k, shape, scale):
        return jax.random.normal(k, shape, f32) * scale

    n_pages = PAST_LEN // PAGE_SIZE
    n_used = DEC_BATCH * n_pages
    n_pool = n_used + n_used // 4
    w_buf = min(WINDOW, PAST_LEN)
    page_table = jax.random.permutation(ks[0], n_pool)[:n_used].reshape(DEC_BATCH, n_pages).astype(jnp.int32)
    return {
        'x_prompt': nrm(ks[1], (BATCH, SEQ, D_MODEL), 1.0),
        'x_sample': nrm(ks[2], (DEC_BATCH, DEC_SEQ, D_MODEL), 1.0),
        'cache_kv': nrm(ks[3], (DEPTH, n_pool, PAGE_SIZE, N_KV_ROWS, KV_HEADS, HEAD_DIM), 1.0),
        'state_win': nrm(ks[4], (DEPTH, DEC_BATCH, w_buf, N_WIN_ROWS, KV_HEADS, HEAD_DIM), 1.0),
        'state_pool': nrm(ks[5], (DEPTH, DEC_BATCH, POOL_KEEP, D_POOL), 1.0),
        'page_table': page_table,
        'norm_mix': 1.0 + nrm(ks[6], (DEPTH, D_MODEL), 0.02),
        'w_in': nrm(ks[7], (DEPTH, D_MODEL, D_IN), D_MODEL ** -0.5),
        'pool_w': nrm(ks[8], (DEPTH, len(POOL_WINDOWS), POOL_GROUP, POOL_GROUP), POOL_GROUP ** -0.5),
        'pool_scale': 1.0 + nrm(ks[9], (DEPTH, D_POOL), 0.02),
        'cmp_w1': nrm(ks[10], (DEPTH, 2, CMP_LEN, HEAD_DIM, CMP_HIDDEN), (CMP_LEN * HEAD_DIM) ** -0.5),
        'cmp_b1': nrm(ks[11], (DEPTH, 2, CMP_HIDDEN), 0.02),
        'cmp_w2': nrm(ks[12], (DEPTH, 2, CMP_HIDDEN, HEAD_DIM), CMP_HIDDEN ** -0.5),
        'w_pool_proj': nrm(ks[13], (DEPTH, D_POOL, D_MODEL), D_POOL ** -0.5),
        'w_attn_proj': nrm(ks[14], (DEPTH, D_ATTN, D_MODEL), D_ATTN ** -0.5),
        'w_out': nrm(ks[15], (DEPTH, D_MODEL, D_MODEL), D_MODEL ** -0.5),
        'norm_ffn': 1.0 + nrm(ks[16], (DEPTH, D_MODEL), 0.02),
        'ffn_wg': nrm(ks[17], (N_DENSE_LAYERS, D_MODEL, D_FF), D_MODEL ** -0.5),
        'ffn_wu': nrm(ks[18], (N_DENSE_LAYERS, D_MODEL, D_FF), D_MODEL ** -0.5),
        'ffn_wd': nrm(ks[19], (N_DENSE_LAYERS, D_FF, D_MODEL), D_FF ** -0.5),
        'moe_router': nrm(ks[20], (N_MOE_LAYERS, D_MODEL, N_EXPERTS), D_MODEL ** -0.5),
        'moe_wg': nrm(ks[21], (N_MOE_LAYERS, N_EXPERTS, D_MODEL, D_FF_EXPERT), D_MODEL ** -0.5),
        'moe_wu': nrm(ks[22], (N_MOE_LAYERS, N_EXPERTS, D_MODEL, D_FF_EXPERT), D_MODEL ** -0.5),
        'moe_wd': nrm(ks[23], (N_MOE_LAYERS, N_EXPERTS, D_FF_EXPERT, D_MODEL), D_FF_EXPERT ** -0.5),
        'norm_final': 1.0 + nrm(ks[24], (D_MODEL,), 0.02),
    }


def reference(x_prompt, x_sample, cache_kv, state_win, state_pool, page_table,
              norm_mix, w_in, pool_w, pool_scale, cmp_w1, cmp_b1, cmp_w2,
              w_pool_proj, w_attn_proj, w_out, norm_ffn, ffn_wg, ffn_wu, ffn_wd,
              moe_router, moe_wg, moe_wu, moe_wd, norm_final):
    y_prompt, kv_prompt, win_prompt, pool_prompt = trunk(
        x_prompt, 0, None, None, None, None,
        norm_mix, w_in, pool_w, pool_scale, cmp_w1, cmp_b1, cmp_w2,
        w_pool_proj, w_attn_proj, w_out, norm_ffn, ffn_wg, ffn_wu, ffn_wd,
        moe_router, moe_wg, moe_wu, moe_wd, norm_final)
    past_len = page_table.shape[1] * PAGE_SIZE
    y_sample, kv_sample, win_sample, pool_sample = trunk(
        x_sample, past_len, cache_kv, page_table, state_win, state_pool,
        norm_mix, w_in, pool_w, pool_scale, cmp_w1, cmp_b1, cmp_w2,
        w_pool_proj, w_attn_proj, w_out, norm_ffn, ffn_wg, ffn_wu, ffn_wd,
        moe_router, moe_wg, moe_wu, moe_wd, norm_final)
    return (y_prompt, y_sample, kv_prompt, kv_sample, win_prompt, win_sample, pool_prompt, pool_sample)
```

```python
import functools
import math

import jax
import jax.numpy as jnp
from jax import lax
from jax.experimental import pallas as pl
from jax.experimental.pallas import tpu as pltpu

f32 = jnp.float32
bf16 = jnp.bfloat16
i32 = jnp.int32

N_HEADS = 16
HEAD_DIM = 128
KV_HEADS = 4
GROUP = N_HEADS // KV_HEADS
CMP_LEN = 32
CMP_STRIDE = 16
SEL_LEN = 64
N_SEL = 16
WINDOW = 512
N_NSA_BRANCH = 3
FORCE = 1e4
ROPE_THETA = 10000.0
POOL_WINDOWS = (2, 4, 8, 16)
POOL_KEEP = max(POOL_WINDOWS) - 1
N_EXPERTS = 8
EPS = 1e-6
NEG = -1e30
SCALE = HEAD_DIM ** -0.5
PAGE_SIZE = 128
LANES = 128
D_KV = KV_HEADS * HEAD_DIM
D_ATTN = N_HEADS * HEAD_DIM
VMEM_LIMIT = 56 * 1024 * 1024

_NT = (((1,), (1,)), ((), ()))


def _cparams(sem):
    return pltpu.CompilerParams(dimension_semantics=sem, vmem_limit_bytes=VMEM_LIMIT)


def _pick(dim, pref, mult=LANES):
    if dim <= pref:
        return dim
    t = (pref // mult) * mult
    while t >= mult:
        if dim % t == 0:
            return t
        t -= mult
    return dim


def _sigmoid(x):
    return 1.0 / (1.0 + jnp.exp(-x))


def _gelu_tanh(x):
    c = math.sqrt(2.0 / math.pi)
    return x * (0.5 * (1.0 + jnp.tanh(c * (x + 0.044715 * (x * x * x)))))


def _rmsnorm_kernel(x_ref, g_ref, o_ref):
    x = x_ref[...]
    ms = jnp.mean(x * x, axis=-1, keepdims=True)
    o_ref[...] = ((x * lax.rsqrt(ms + EPS)) * g_ref[...]).astype(o_ref.dtype)


def rmsnorm(x, gains, layer, out_dtype):
    M, D = x.shape
    tm = _pick(M, 512, 8)
    g3 = gains.reshape(gains.shape[0], 1, D)
    return pl.pallas_call(
        _rmsnorm_kernel,
        grid=(M // tm,),
        in_specs=[pl.BlockSpec((tm, D), lambda i: (i, 0)),
                  pl.BlockSpec((None, 1, D), lambda i: (layer, 0, 0))],
        out_specs=pl.BlockSpec((tm, D), lambda i: (i, 0)),
        out_shape=jax.ShapeDtypeStruct((M, D), out_dtype),
        compiler_params=_cparams(("parallel",)),
        name="rmsnorm",
    )(x, g3)


def _mm_kernel(*refs, n_w, n_extra, nk, epilogue):
    x_ref = refs[0]
    w_refs = refs[1:1 + n_w]
    extras = refs[1 + n_w:1 + n_w + n_extra]
    o_ref = refs[1 + n_w + n_extra]
    acc_refs = refs[2 + n_w + n_extra:]
    j = pl.program_id(1)
    k = pl.program_id(2)
    x = x_ref[...].astype(bf16)
    parts = [jnp.dot(x, w_ref[...].astype(bf16), preferred_element_type=f32) for w_ref in w_refs]
    if nk == 1:
        epilogue(parts, extras, o_ref, j)
        return

    @pl.when(k == 0)
    def _():
        for a, p in zip(acc_refs, parts):
            a[...] = p

    @pl.when(k > 0)
    def _():
        for a, p in zip(acc_refs, parts):
            a[...] += p

    @pl.when(k == nk - 1)
    def _():
        epilogue([a[...] for a in acc_refs], extras, o_ref, j)


def _matmul(x, ws, w_lead, n_cols, col_off_blocks, extras, extra_specs, epilogue, out_dtype,
            tm_pref=1024, tn_pref=512, tk_pref=2048, name="matmul"):
    M, K = x.shape
    tm = _pick(M, tm_pref, 8)
    tn = _pick(n_cols, tn_pref)
    tk = _pick(K, tk_pref)
    nk = K // tk
    n_lead = len(w_lead)
    w_spec = pl.BlockSpec((None,) * n_lead + (tk, tn),
                          lambda i, j, k: tuple(w_lead) + (k, j + col_off_blocks))
    kern = functools.partial(_mm_kernel, n_w=len(ws), n_extra=len(extras), nk=nk, epilogue=epilogue)
    scratch = [pltpu.VMEM((tm, tn), f32) for _ in ws] if nk > 1 else []
    return pl.pallas_call(
        kern,
        grid=(M // tm, n_cols // tn, nk),
        in_specs=[pl.BlockSpec((tm, tk), lambda i, j, k: (i, k))] + [w_spec] * len(ws) + list(extra_specs(tm, tn)),
        out_specs=pl.BlockSpec((tm, tn), lambda i, j, k: (i, j)),
        out_shape=jax.ShapeDtypeStruct((M, n_cols), out_dtype),
        scratch_shapes=scratch,
        compiler_params=_cparams(("parallel", "parallel", "arbitrary")),
        name=name,
    )(x, *ws, *extras)


def _no_extras(tm, tn):
    return []


def _ep_plain(accs, extras, o_ref, j):
    o_ref[...] = accs[0].astype(o_ref.dtype)


def _ep_residual(accs, extras, o_ref, j):
    o_ref[...] = (extras[0][...] + accs[0]).astype(o_ref.dtype)


def _ep_swiglu(accs, extras, o_ref, j):
    g, u = accs
    o_ref[...] = ((g * _sigmoid(g)) * u).astype(o_ref.dtype)


def _ep_merge(accs, extras, o_ref, j):
    p_ref, g0_ref, g1_ref = extras
    h = _sigmoid(g0_ref[...]) * p_ref[...] + _sigmoid(g1_ref[...]) * accs[0]
    o_ref[...] = h.astype(o_ref.dtype)


def _rope_tile(v, cos, sin):
    outs = []
    for h in range(v.shape[1] // HEAD_DIM):
        seg = v[:, h * HEAD_DIM:(h + 1) * HEAD_DIM]
        outs.append(seg * cos + pltpu.roll(seg, HEAD_DIM // 2, axis=1) * sin)
    return jnp.concatenate(outs, axis=1) if len(outs) > 1 else outs[0]


def _ep_inproj(accs, extras, o_ref, j, *, rope_blocks):
    cos_ref, sin_ref = extras
    flag = functools.reduce(jnp.logical_or, [j == b for b in rope_blocks])

    @pl.when(flag)
    def _():
        o_ref[...] = _rope_tile(accs[0], cos_ref[...], sin_ref[...]).astype(o_ref.dtype)

    @pl.when(jnp.logical_not(flag))
    def _():
        o_ref[...] = accs[0].astype(o_ref.dtype)


def _ep_router(accs, extras, o_ref, j):
    logits = accs[0]
    lane = lax.broadcasted_iota(i32, logits.shape, 1)
    lg = jnp.where(lane < N_EXPERTS, logits, -jnp.inf)
    m1 = jnp.max(lg, axis=-1, keepdims=True)
    i1 = jnp.min(jnp.where(lg == m1, lane, LANES), axis=-1, keepdims=True)
    lg2 = jnp.where(lane == i1, -jnp.inf, lg)
    m2 = jnp.max(lg2, axis=-1, keepdims=True)
    i2 = jnp.min(jnp.where(lg2 == m2, lane, LANES), axis=-1, keepdims=True)
    e2 = jnp.exp(m2 - m1)
    den = 1.0 + e2
    comb = jnp.where(lane == i1, 1.0 / den, 0.0) + jnp.where(lane == i2, e2 / den, 0.0)
    o_ref[...] = comb.astype(o_ref.dtype)


def _ep_moe_acc(accs, extras, o_ref, j, *, expert, first, last):
    lane_src = extras[0][...]
    lane = lax.broadcasted_iota(i32, lane_src.shape, 1)
    c = jnp.sum(jnp.where(lane == expert, lane_src, 0.0), axis=-1, keepdims=True)
    out = c * accs[0]
    nxt = 1
    if not first:
        out = extras[nxt][...] + out
        nxt += 1
    if last:
        out = extras[nxt][...] + out
    o_ref[...] = out.astype(o_ref.dtype)


def _pool_groups(shifted, u, pos1, pw_ref, sc_ref, o_ref):
    pg = u.shape[1] // len(POOL_WINDOWS)
    for g, w in enumerate(POOL_WINDOWS):
        c = slice(g * pg, (g + 1) * pg)
        s = u[:, c]
        for jj in range(1, w):
            s = s + shifted(jj, c)
        cnt = jnp.minimum(float(w), pos1)
        pooled = s / cnt - u[:, c]
        mixed = jnp.dot(pooled.astype(bf16), pw_ref[g].astype(bf16), preferred_element_type=f32)
        o_ref[:, c] = (mixed * sc_ref[:, c]).astype(o_ref.dtype)


def _pool_prompt_kernel(u_ref, pre_ref, pw_ref, sc_ref, o_ref, ext_ref, *, tp, t0):
    i = pl.program_id(1)
    halo = POOL_KEEP + 1

    @pl.when(i == 0)
    def _():
        ext_ref[0:halo, :] = pre_ref[...]

    u = u_ref[...]
    ext_ref[halo:halo + tp, :] = u
    pos1 = (lax.broadcasted_iota(i32, (tp, 1), 0) + (i * tp + t0 + 1)).astype(f32)
    _pool_groups(lambda jj, c: ext_ref[halo - jj:halo - jj + tp, c], u, pos1, pw_ref, sc_ref, o_ref)
    ext_ref[0:halo, :] = ext_ref[tp:tp + halo, :]


def pool_prompt(proj3, prefix16, pool_w, pool_scale, layer, t0):
    B, T, _ = proj3.shape
    dp = pool_scale.shape[-1]
    tp = _pick(T, 512, 16)
    kern = functools.partial(_pool_prompt_kernel, tp=tp, t0=t0)
    sc3 = pool_scale.reshape(pool_scale.shape[0], 1, dp)
    return pl.pallas_call(
        kern,
        grid=(B, T // tp),
        in_specs=[pl.BlockSpec((None, tp, dp), lambda b, i: (b, i, 0)),
                  pl.BlockSpec((None, POOL_KEEP + 1, dp), lambda b, i: (b, 0, 0)),
                  pl.BlockSpec((None,) + pool_w.shape[1:], lambda b, i: (layer, 0, 0, 0)),
                  pl.BlockSpec((None, 1, dp), lambda b, i: (layer, 0, 0))],
        out_specs=pl.BlockSpec((None, tp, dp), lambda b, i: (b, i, 0)),
        out_shape=jax.ShapeDtypeStruct((B, T, dp), bf16),
        scratch_shapes=[pltpu.VMEM((tp + POOL_KEEP + 1, dp), f32)],
        compiler_params=_cparams(("arbitrary", "arbitrary")),
        name="pool_prompt",
    )(proj3, prefix16, pool_w, sc3)


def _pool_sample_kernel(u_ref, st_ref, pw_ref, sc_ref, o_ref, *, t0):
    u = u_ref[...]
    pos1 = jnp.full((u.shape[0], 1), float(t0 + 1), f32)
    _pool_groups(lambda jj, c: st_ref[POOL_KEEP - jj, :, c], u, pos1, pw_ref, sc_ref, o_ref)


def pool_sample(proj, state_t, pool_w, pool_scale, layer, t0):
    B = proj.shape[0]
    dp = pool_scale.shape[-1]
    kern = functools.partial(_pool_sample_kernel, t0=t0)
    sc3 = pool_scale.reshape(pool_scale.shape[0], 1, dp)
    return pl.pallas_call(
        kern,
        grid=(1,),
        in_specs=[pl.BlockSpec((B, dp), lambda i: (0, 0)),
                  pl.BlockSpec((POOL_KEEP, B, dp), lambda i: (0, 0, 0)),
                  pl.BlockSpec((None,) + pool_w.shape[1:], lambda i: (layer, 0, 0, 0)),
                  pl.BlockSpec((None, 1, dp), lambda i: (layer, 0, 0))],
        out_specs=pl.BlockSpec((B, dp), lambda i: (0, 0)),
        out_shape=jax.ShapeDtypeStruct((B, dp), bf16),
        compiler_params=_cparams(("arbitrary",)),
        name="pool_sample",
    )(proj, state_t, pool_w, sc3)


def _compress_tail(h01, b1, w2):
    n = h01.shape[0]
    hd = h01.shape[1] // 2
    hid = h01[:, :hd] + pltpu.roll(h01[:, hd:], n - 1, axis=0) + b1
    return jnp.dot(_gelu_tanh(hid).astype(bf16), w2.astype(bf16), preferred_element_type=f32)


def _compress_prompt_kernel(x_ref, wc_ref, b1_ref, w2_ref, o_ref, *, n_sub):
    acc = None
    for l in range(CMP_STRIDE):
        xl = x_ref[pl.ds(l, n_sub, stride=CMP_STRIDE), :].astype(bf16)
        p = jnp.dot(xl, wc_ref[l].astype(bf16), preferred_element_type=f32)
        acc = p if acc is None else acc + p
    o_ref[...] = _compress_tail(acc, b1_ref[...], w2_ref[...]).astype(o_ref.dtype)


def compress_prompt(proj3, wc, b1, w2, layer, kv_col_block):
    B, T, _ = proj3.shape
    n_sub = T // CMP_STRIDE
    hid2 = wc.shape[-1]
    kern = functools.partial(_compress_prompt_kernel, n_sub=n_sub)
    return pl.pallas_call(
        kern,
        grid=(B, 2, KV_HEADS),
        in_specs=[pl.BlockSpec((None, T, HEAD_DIM), lambda b, r, h: (b, 0, kv_col_block + r * KV_HEADS + h)),
                  pl.BlockSpec((None, None, CMP_STRIDE, HEAD_DIM, hid2), lambda b, r, h: (layer, r, 0, 0, 0)),
                  pl.BlockSpec((None, None, 1, hid2 // 2), lambda b, r, h: (layer, r, 0, 0)),
                  pl.BlockSpec((None, None, hid2 // 2, HEAD_DIM), lambda b, r, h: (layer, r, 0, 0))],
        out_specs=pl.BlockSpec((None, None, None, n_sub, HEAD_DIM), lambda b, r, h: (b, r, h, 0, 0)),
        out_shape=jax.ShapeDtypeStruct((B, 2, KV_HEADS, n_sub, HEAD_DIM), bf16),
        compiler_params=_cparams(("parallel", "parallel", "parallel")),
        name="compress_prompt",
    )(proj3, wc, b1, w2)


def _compress_pages_kernel(pt_ref, *refs, pg):
    page_refs = refs[:pg]
    wc_ref = refs[pg]
    o_ref = refs[pg + 1]
    head_ref = refs[pg + 2]
    sub = PAGE_SIZE // CMP_STRIDE
    for s in range(pg):
        for rh in range(2 * KV_HEADS):
            head_ref[rh, s * PAGE_SIZE:(s + 1) * PAGE_SIZE, :] = page_refs[s][:, rh * HEAD_DIM:(rh + 1) * HEAD_DIM]
    for r in range(2):
        acc = None
        for l in range(CMP_STRIDE):
            rows = [head_ref[r * KV_HEADS + h, pl.ds(l, pg * sub, stride=CMP_STRIDE), :] for h in range(KV_HEADS)]
            xl = jnp.concatenate(rows, axis=0).astype(bf16)
            p = jnp.dot(xl, wc_ref[r, l].astype(bf16), preferred_element_type=f32)
            acc = p if acc is None else acc + p
        for h in range(KV_HEADS):
            o_ref[r, h] = acc[h * pg * sub:(h + 1) * pg * sub]


def compress_pages(cache_pages, page_table, wc, layer, n_pool):
    B, n_pages = page_table.shape
    pg = math.gcd(n_pages, 8)
    sub = PAGE_SIZE // CMP_STRIDE
    hid2 = wc.shape[-1]
    kern = functools.partial(_compress_pages_kernel, pg=pg)

    def page_spec(s):
        return pl.BlockSpec((None, PAGE_SIZE, 2 * D_KV),
                            lambda b, p, pt: (layer * n_pool + pt[b, p * pg + s], 0, 0))

    grid_spec = pltpu.PrefetchScalarGridSpec(
        num_scalar_prefetch=1,
        grid=(B, n_pages // pg),
        in_specs=[page_spec(s) for s in range(pg)]
        + [pl.BlockSpec((None, 2, CMP_STRIDE, HEAD_DIM, hid2), lambda b, p, pt: (layer, 0, 0, 0, 0))],
        out_specs=pl.BlockSpec((None, 2, KV_HEADS, pg * sub, hid2), lambda b, p, pt: (b, 0, 0, p, 0)),
        scratch_shapes=[pltpu.VMEM((2 * KV_HEADS, pg * PAGE_SIZE, HEAD_DIM), f32)],
    )
    return pl.pallas_call(
        kern,
        grid_spec=grid_spec,
        out_shape=jax.ShapeDtypeStruct((B, 2, KV_HEADS, n_pages * sub, hid2), f32),
        compiler_params=_cparams(("parallel", "parallel")),
        name="compress_pages",
    )(page_table, *([cache_pages] * pg), wc)


def _compress_finish_kernel(h_ref, b1_ref, w2_ref, o_ref):
    for h in range(KV_HEADS):
        o_ref[h] = _compress_tail(h_ref[h], b1_ref[...], w2_ref[...]).astype(o_ref.dtype)


def compress_finish(hbuf, b1, w2, layer):
    B, _, _, n_sub, hid2 = hbuf.shape
    return pl.pallas_call(
        _compress_finish_kernel,
        grid=(B, 2),
        in_specs=[pl.BlockSpec((None, None, KV_HEADS, n_sub, hid2), lambda b, r: (b, r, 0, 0, 0)),
                  pl.BlockSpec((None, None, 1, hid2 // 2), lambda b, r: (layer, r, 0, 0)),
                  pl.BlockSpec((None, None, hid2 // 2, HEAD_DIM), lambda b, r: (layer, r, 0, 0))],
        out_specs=pl.BlockSpec((None, None, KV_HEADS, n_sub, HEAD_DIM), lambda b, r: (b, r, 0, 0, 0)),
        out_shape=jax.ShapeDtypeStruct((B, 2, KV_HEADS, n_sub, HEAD_DIM), bf16),
        compiler_params=_cparams(("parallel", "parallel")),
        name="compress_finish",
    )(hbuf, b1, w2)


def _masked_softmax_rows(s, mask):
    s = jnp.where(mask, s, NEG)
    m = jnp.max(s, axis=-1, keepdims=True)
    e = jnp.where(mask, jnp.exp(s - m), 0.0)
    l = jnp.sum(e, axis=-1, keepdims=True)
    return e * jnp.where(l > 0.0, 1.0 / l, 0.0)


def _nsa_prompt_kernel(q_ref, gn_ref, kc_ref, vc_ref, ks_ref, vs_ref, kw_ref, vw_ref, ov_ref, ex_ref,
                       o_ref, m_sc, l_sc, acc_sc, *, tq, n_blk, kts, ktw):
    kh = pl.program_id(1)
    i = pl.program_id(2)
    t0 = i * tq
    rows = GROUP * tq
    q_all = jnp.concatenate([q_ref[:, g * HEAD_DIM:(g + 1) * HEAD_DIM] for g in range(GROUP)], axis=0).astype(bf16)
    t_q = t0 + lax.broadcasted_iota(i32, (tq, 1), 0)
    t_rows = jnp.concatenate([t_q] * GROUP, axis=0)

    n_cp = kc_ref.shape[0]
    s = lax.dot_general(q_all, kc_ref[...], _NT, preferred_element_type=f32) * SCALE
    cmp_end = lax.broadcasted_iota(i32, (1, n_cp), 1) * CMP_STRIDE + (CMP_LEN - 1)
    p_c = _masked_softmax_rows(s, cmp_end <= t_rows).astype(bf16)
    o_c = jnp.dot(p_c, vc_ref[...], preferred_element_type=f32)

    imp_g = jnp.dot(p_c, ov_ref[...], preferred_element_type=f32)
    imp = imp_g[0:tq]
    for g in range(1, GROUP):
        imp = imp + imp_g[g * tq:(g + 1) * tq]
    nbp = imp.shape[1]
    blk = lax.broadcasted_iota(i32, (1, nbp), 1)
    cur = t_q // SEL_LEN
    valid = blk * SEL_LEN <= t_q
    forced = (blk == 0) | (blk == cur) | (blk == cur - 1)
    imp = jnp.where(valid, jnp.where(forced, FORCE, imp), -jnp.inf)
    rank = jnp.zeros(imp.shape, f32)
    for ii in range(n_blk):
        col = imp[:, ii:ii + 1]
        beats = (col > imp) | ((col == imp) & (blk > ii))
        rank = rank + jnp.where(beats, 1.0, 0.0)
    sel = jnp.where((rank < float(N_SEL)) & valid, 1.0, 0.0).astype(bf16)

    def flash(k_ref, v_ref, lo, hi, kt_size, mask_fn):
        m_sc[...] = jnp.full(m_sc.shape, NEG, f32)
        l_sc[...] = jnp.zeros(l_sc.shape, f32)
        acc_sc[...] = jnp.zeros(acc_sc.shape, f32)

        def body(kt, carry):
            off = pl.multiple_of(kt * kt_size, kt_size)
            k_t = k_ref[pl.ds(off, kt_size), :].astype(bf16)
            v_t = v_ref[pl.ds(off, kt_size), :].astype(bf16)
            sc = lax.dot_general(q_all, k_t, _NT, preferred_element_type=f32) * SCALE
            kpos = off + lax.broadcasted_iota(i32, (1, kt_size), 1)
            mq = mask_fn(kt, kpos)
            mk = jnp.concatenate([mq] * GROUP, axis=0)
            sc = jnp.where(mk > 0.5, sc, NEG)
            m_old = m_sc[...]
            m_new = jnp.maximum(m_old, jnp.max(sc, axis=-1, keepdims=True))
            alpha = jnp.exp(m_old - m_new)
            e = jnp.exp(sc - m_new) * mk
            l_sc[...] = alpha * l_sc[...] + jnp.sum(e, axis=-1, keepdims=True)
            acc_sc[...] = alpha * acc_sc[...] + jnp.dot(e.astype(bf16), v_t, preferred_element_type=f32)
            m_sc[...] = m_new
            return carry

        lax.fori_loop(lo, hi, body, 0)
        l = l_sc[...]
        return acc_sc[...] * jnp.where(l > 0.0, 1.0 / l, 0.0)

    def sel_mask(kt, kpos):
        hit = jnp.dot(sel, ex_ref[kt], preferred_element_type=f32)
        return jnp.where((hit > 0.5) & (kpos <= t_q), 1.0, 0.0)

    def win_mask(kt, kpos):
        return jnp.where((kpos <= t_q) & (kpos > t_q - WINDOW), 1.0, 0.0)

    o_s = flash(ks_ref, vs_ref, 0, (t0 + tq - 1) // kts + 1, kts, sel_mask)
    o_w = flash(kw_ref, vw_ref, jnp.maximum(t0 - WINDOW, 0) // ktw, (t0 + tq - 1) // ktw + 1, ktw, win_mask)

    sig = _sigmoid(gn_ref[...])
    lane = lax.broadcasted_iota(i32, sig.shape, 1)
    for g in range(GROUP):
        r = slice(g * tq, (g + 1) * tq)
        out = None
        for br, o_br in enumerate((o_c, o_s, o_w)):
            colid = kh * (GROUP * N_NSA_BRANCH) + g * N_NSA_BRANCH + br
            gate = jnp.sum(jnp.where(lane == colid, sig, 0.0), axis=-1, keepdims=True)
            term = gate * o_br[r]
            out = term if out is None else out + term
        o_ref[:, g * HEAD_DIM:(g + 1) * HEAD_DIM] = out.astype(o_ref.dtype)


def _cmp_to_sel(n_rows, n_blk, n_cols):
    i0 = jnp.arange(n_rows)[:, None] * CMP_STRIDE
    j0 = jnp.arange(n_cols)[None, :] * SEL_LEN
    ov = jnp.clip(jnp.minimum(i0 + CMP_LEN, j0 + SEL_LEN) - jnp.maximum(i0, j0), 0, None)
    ov = jnp.where(jnp.arange(n_cols)[None, :] < n_blk, ov, 0)
    return (ov.astype(f32) / CMP_LEN).astype(bf16)


def nsa_prompt(proj3, gn3, kc, q_col, kv_col):
    B, T, _ = proj3.shape
    assert T % SEL_LEN == 0
    tq = _pick(T, 128, 64)
    kts = _pick(T, 256, 64)
    ktw = _pick(T, 128, 64)
    n_blk = T // SEL_LEN
    nbp = -(-n_blk // LANES) * LANES
    n_cp = T // CMP_STRIDE
    ov = _cmp_to_sel(n_cp, n_blk, nbp)
    tok_blk = jnp.arange(T) // SEL_LEN
    ex = (jnp.arange(nbp)[:, None] == tok_blk[None, :]).astype(bf16)
    ex = ex.reshape(nbp, T // kts, kts).transpose(1, 0, 2)
    rows = GROUP * tq
    kern = functools.partial(_nsa_prompt_kernel, tq=tq, n_blk=n_blk, kts=kts, ktw=ktw)

    def kv_spec(r):
        return pl.BlockSpec((None, T, HEAD_DIM), lambda b, k, i: (b, 0, kv_col + r * KV_HEADS + k))

    def cmp_spec(r):
        return pl.BlockSpec((None, None, None, n_cp, HEAD_DIM), lambda b, k, i: (b, r, k, 0, 0))

    return pl.pallas_call(
        kern,
        grid=(B, KV_HEADS, T // tq),
        in_specs=[pl.BlockSpec((None, tq, GROUP * HEAD_DIM), lambda b, k, i: (b, i, q_col + k)),
                  pl.BlockSpec((None, tq, LANES), lambda b, k, i: (b, i, 0)),
                  cmp_spec(0), cmp_spec(1), kv_spec(2), kv_spec(3), kv_spec(4), kv_spec(5),
                  pl.BlockSpec((n_cp, nbp), lambda b, k, i: (0, 0)),
                  pl.BlockSpec((T // kts, nbp, kts), lambda b, k, i: (0, 0, 0))],
        out_specs=pl.BlockSpec((None, tq, GROUP * HEAD_DIM), lambda b, k, i: (b, i, k)),
        out_shape=jax.ShapeDtypeStruct((B, T, D_ATTN), bf16),
        scratch_shapes=[pltpu.VMEM((rows, 1), f32), pltpu.VMEM((rows, 1), f32), pltpu.VMEM((rows, HEAD_DIM), f32)],
        compiler_params=_cparams(("parallel", "parallel", "arbitrary")),
        name="nsa_prompt",
    )(proj3, gn3, kc, kc, proj3, proj3, proj3, proj3, ov, ex)


def _group_rows(n_rows):
    return lax.broadcasted_iota(i32, (n_rows, 1), 0) // GROUP


def _nsa_select_kernel(q_ref, kc_ref, vc_ref, ov_ref, oc_ref, sel_ref, *, t_pos, n_blk):
    q = q_ref[...].astype(bf16)
    n_cp = kc_ref.shape[1]
    nbp = ov_ref.shape[1]
    grp = _group_rows(N_HEADS)
    cmp_end = lax.broadcasted_iota(i32, (1, n_cp), 1) * CMP_STRIDE + (CMP_LEN - 1)
    blk = lax.broadcasted_iota(i32, (1, nbp), 1)
    cur = t_pos // SEL_LEN
    valid = (blk * SEL_LEN <= t_pos) & (blk < n_blk)
    forced = (blk == 0) | (blk == cur) | (blk == cur - 1)
    ri = lax.broadcasted_iota(i32, (nbp, nbp), 0)
    ci = lax.broadcasted_iota(i32, (nbp, nbp), 1)
    o_c = jnp.zeros((N_HEADS, HEAD_DIM), f32)
    for k in range(KV_HEADS):
        s = lax.dot_general(q, kc_ref[k], _NT, preferred_element_type=f32) * SCALE
        p = _masked_softmax_rows(s, cmp_end <= t_pos).astype(bf16)
        o_k = jnp.dot(p, vc_ref[k], preferred_element_type=f32)
        o_c = o_c + jnp.where(grp == k, o_k, 0.0)
        imp_h = jnp.dot(p, ov_ref[...], preferred_element_type=f32)
        imp = jnp.sum(jnp.where(grp == k, imp_h, 0.0), axis=0, keepdims=True)
        imp = jnp.where(valid, jnp.where(forced, FORCE, imp), -jnp.inf)
        a = jnp.broadcast_to(imp, (nbp, nbp))
        at = jnp.broadcast_to(jnp.sum(jnp.where(ri == ci, a, 0.0), axis=1, keepdims=True), (nbp, nbp))
        beats = (at > a) | ((at == a) & (ri < ci))
        rank = jnp.sum(jnp.where(beats, 1.0, 0.0), axis=0, keepdims=True)
        sel_ref[k:k + 1, :] = jnp.where((rank < float(N_SEL)) & valid, 1.0, 0.0)
    oc_ref[...] = o_c


def nsa_select(q3, kc, ov, t_pos, n_blk):
    B = q3.shape[0]
    n_cp = kc.shape[3]
    nbp = ov.shape[1]
    kern = functools.partial(_nsa_select_kernel, t_pos=t_pos, n_blk=n_blk)
    return pl.pallas_call(
        kern,
        grid=(B,),
        in_specs=[pl.BlockSpec((None, N_HEADS, HEAD_DIM), lambda b: (b, 0, 0)),
                  pl.BlockSpec((None, None, KV_HEADS, n_cp, HEAD_DIM), lambda b: (b, 0, 0, 0, 0)),
                  pl.BlockSpec((None, None, KV_HEADS, n_cp, HEAD_DIM), lambda b: (b, 1, 0, 0, 0)),
                  pl.BlockSpec((n_cp, nbp), lambda b: (0, 0))],
        out_specs=[pl.BlockSpec((None, N_HEADS, HEAD_DIM), lambda b: (b, 0, 0)),
                   pl.BlockSpec((None, KV_HEADS, nbp), lambda b: (b, 0, 0))],
        out_shape=[jax.ShapeDtypeStruct((B, N_HEADS, HEAD_DIM), f32),
                   jax.ShapeDtypeStruct((B, KV_HEADS, nbp), f32)],
        compiler_params=_cparams(("parallel",)),
        name="nsa_select",
    )(q3, kc, kc, ov)


def _gather_copies(tok_ref, cache_ref, kbuf, vbuf, sems, b, n_gather):
    copies = []
    for k in range(KV_HEADS):
        for s in range(n_gather):
            off = pl.multiple_of(tok_ref[(b * KV_HEADS + k) * n_gather + s], SEL_LEN)
            for r, buf, sem in ((2, kbuf, sems.at[0]), (3, vbuf, sems.at[1])):
                src = cache_ref.at[pl.ds(off, SEL_LEN), pl.ds((r * KV_HEADS + k) * HEAD_DIM, HEAD_DIM)]
                copies.append(pltpu.make_async_copy(src, buf.at[k, pl.ds(s * SEL_LEN, SEL_LEN)], sem))
    return copies


def _nsa_decode_kernel(tok_ref, cache_ref, q_ref, oc_ref, gn_ref, new_ref, win_ref, o_ref, kbuf, vbuf, sems,
                       *, n_gather):
    b = pl.program_id(0)
    n_keys = (n_gather + 1) * SEL_LEN
    new_col = n_gather * SEL_LEN
    for k in range(KV_HEADS):
        kbuf[k, new_col:n_keys, :] = jnp.zeros((SEL_LEN, HEAD_DIM), f32)
        vbuf[k, new_col:n_keys, :] = jnp.zeros((SEL_LEN, HEAD_DIM), f32)
    copies = _gather_copies(tok_ref, cache_ref, kbuf, vbuf, sems, b, n_gather)
    for c in copies:
        c.start()
    for c in copies:
        c.wait()

    q = q_ref[...].astype(bf16)
    qf = q.astype(f32)
    grp = _group_rows(N_HEADS)
    lane_s = lax.broadcasted_iota(i32, (1, n_keys), 1)
    n_win = win_ref.shape[0]
    lane_w = lax.broadcasted_iota(i32, (1, n_win), 1)
    o_s = jnp.zeros((N_HEADS, HEAD_DIM), f32)
    o_w = jnp.zeros((N_HEADS, HEAD_DIM), f32)

    def attend(keys, vals, k_new, v_new, lane, slot, mask):
        s = lax.dot_general(q, keys.astype(bf16), _NT, preferred_element_type=f32)
        s_new = jnp.sum(qf * k_new.astype(bf16).astype(f32), axis=-1, keepdims=True)
        s = jnp.where(lane == slot, s_new, s) * SCALE
        p = _masked_softmax_rows(s, mask).astype(bf16)
        p_new = jnp.sum(jnp.where(lane == slot, p.astype(f32), 0.0), axis=-1, keepdims=True)
        p_old = jnp.where(lane == slot, 0.0, p.astype(f32)).astype(bf16)
        return (jnp.dot(p_old, vals.astype(bf16), preferred_element_type=f32)
                + p_new * v_new.astype(bf16).astype(f32))

    for k in range(KV_HEADS):
        o_k = attend(kbuf[k], vbuf[k], new_ref[2 * KV_HEADS + k:2 * KV_HEADS + k + 1, :],
                     new_ref[3 * KV_HEADS + k:3 * KV_HEADS + k + 1, :], lane_s, new_col, lane_s <= new_col)
        o_s = o_s + jnp.where(grp == k, o_k, 0.0)
        o_k = attend(win_ref[:, k * HEAD_DIM:(k + 1) * HEAD_DIM],
                     win_ref[:, D_KV + k * HEAD_DIM:D_KV + (k + 1) * HEAD_DIM],
                     new_ref[4 * KV_HEADS + k:4 * KV_HEADS + k + 1, :],
                     new_ref[5 * KV_HEADS + k:5 * KV_HEADS + k + 1, :], lane_w, 0, lane_w >= 0)
        o_w = o_w + jnp.where(grp == k, o_k, 0.0)

    sig = jnp.broadcast_to(_sigmoid(gn_ref[...]), (N_HEADS, LANES))
    lane = lax.broadcasted_iota(i32, (N_HEADS, LANES), 1)
    head = lax.broadcasted_iota(i32, (N_HEADS, LANES), 0)
    out = None
    for br, o_br in enumerate((oc_ref[...], o_s, o_w)):
        gate = jnp.sum(jnp.where(lane == head * N_NSA_BRANCH + br, sig, 0.0), axis=-1, keepdims=True)
        term = gate * o_br
        out = term if out is None else out + term
    o_ref[...] = out.astype(o_ref.dtype)


def nsa_decode(tok_off, cache_rows, q3, o_c, gn3, new3, state_win4, layer, n_gather):
    B = q3.shape[0]
    n_win = state_win4.shape[2]
    n_keys = (n_gather + 1) * SEL_LEN
    kern = functools.partial(_nsa_decode_kernel, n_gather=n_gather)
    grid_spec = pltpu.PrefetchScalarGridSpec(
        num_scalar_prefetch=1,
        grid=(B,),
        in_specs=[pl.BlockSpec(memory_space=pl.ANY),
                  pl.BlockSpec((None, N_HEADS, HEAD_DIM), lambda b, t: (b, 0, 0)),
                  pl.BlockSpec((None, N_HEADS, HEAD_DIM), lambda b, t: (b, 0, 0)),
                  pl.BlockSpec((None, 1, LANES), lambda b, t: (b, 0, 0)),
                  pl.BlockSpec((None, 6 * KV_HEADS, HEAD_DIM), lambda b, t: (b, 0, 0)),
                  pl.BlockSpec((None, None, n_win, 2 * D_KV), lambda b, t: (layer, b, 0, 0))],
        out_specs=pl.BlockSpec((None, N_HEADS, HEAD_DIM), lambda b, t: (b, 0, 0)),
        scratch_shapes=[pltpu.VMEM((KV_HEADS, n_keys, HEAD_DIM), f32),
                        pltpu.VMEM((KV_HEADS, n_keys, HEAD_DIM), f32),
                        pltpu.SemaphoreType.DMA((2,))],
    )
    return pl.pallas_call(
        kern,
        grid_spec=grid_spec,
        out_shape=jax.ShapeDtypeStruct((B, N_HEADS, HEAD_DIM), bf16),
        compiler_params=_cparams(("arbitrary",)),
        name="nsa_decode",
    )(tok_off, cache_rows, q3, o_c, gn3, new3, state_win4)


def _rope_tables(pos):
    half = HEAD_DIM // 2
    inv = jnp.power(ROPE_THETA, -jnp.arange(0, HEAD_DIM, 2, dtype=f32) / HEAD_DIM)
    ang = pos.astype(f32)[:, None] * inv[None, :]
    cos, sin = jnp.cos(ang), jnp.sin(ang)
    return jnp.concatenate([cos, cos], axis=-1), jnp.concatenate([-sin, sin], axis=-1)


def _trunk(x, t0, past, p):
    B, T, D = x.shape
    M = B * T
    depth = p["w_in"].shape[0]
    d_pool = p["pool_scale"].shape[-1]
    d_main = d_pool + D_ATTN + 6 * D_KV
    gate_col = d_main
    mrg_col = d_main + N_HEADS * N_NSA_BRANCH
    tn_main = 512
    assert d_pool % tn_main == 0 and d_main % tn_main == 0 and gate_col % LANES == 0
    q_blk0 = d_pool // tn_main
    kv_blk0 = (d_pool + D_ATTN) // tn_main
    rope_blocks = tuple(range(q_blk0, kv_blk0)) + tuple(kv_blk0 + r for r in (0, 2, 4))
    kv_col = (d_pool + D_ATTN) // HEAD_DIM

    cos, sin = _rope_tables(t0 + jnp.arange(T))
    tm = _pick(M, 1024, 8)
    if T >= tm:
        n_tab = T // tm
    else:
        cos, sin = jnp.tile(cos, (tm // T, 1)), jnp.tile(sin, (tm // T, 1))
        n_tab = 1

    def rope_specs(tm_, tn_):
        assert tm_ == tm
        spec = pl.BlockSpec((tm, HEAD_DIM), lambda i, j, k: (i % n_tab, 0))
        return [spec, spec]

    x2 = x.reshape(M, D)
    w_mrg = p["w_in"][:, :, mrg_col:]
    wc = jnp.concatenate([p["cmp_w1"][:, :, :CMP_STRIDE], p["cmp_w1"][:, :, CMP_STRIDE:]], axis=-1)
    cb1 = p["cmp_b1"][:, :, None, :]
    if past is not None:
        cache_kv, page_table, state_win, state_pool = past
        n_pool = cache_kv.shape[1]
        n_pages = page_table.shape[1]
        past_len = n_pages * PAGE_SIZE
        assert T == 1 and t0 == past_len and state_win.shape[2] == WINDOW
        n_blk = past_len // SEL_LEN + 1
        assert n_blk >= N_SEL
        nbp = -(-n_blk // LANES) * LANES
        cache_pages = cache_kv.reshape(depth * n_pool, PAGE_SIZE, 4 * D_KV)
        cache_rows = cache_kv.reshape(depth * n_pool * PAGE_SIZE, 4 * D_KV)
        state_win4 = state_win.reshape(depth, B, WINDOW, 2 * D_KV)
        ov_s = _cmp_to_sel(past_len // CMP_STRIDE, n_blk, nbp)

    kv_new, win_new, pool_new = [], [], []
    for l in range(depth):
        xn = rmsnorm(x2, p["norm_mix"], l, bf16)
        proj = _matmul(xn, [p["w_in"]], (l,), d_main, 0, [cos, sin], rope_specs,
                       functools.partial(_ep_inproj, rope_blocks=rope_blocks), f32,
                       tm_pref=tm, tn_pref=tn_main, name="in_proj")
        gn = _matmul(xn, [p["w_in"]], (l,), LANES, gate_col // LANES, [], _no_extras, _ep_plain, f32,
                     tm_pref=tm, tn_pref=LANES, name="gate_proj")
        g_mrg = _matmul(xn, [w_mrg], (l,), 2 * D, 0, [], _no_extras, _ep_plain, f32, tm_pref=tm, name="merge_gate_proj")

        rows = proj[:, d_pool + D_ATTN:d_pool + D_ATTN + 4 * D_KV].reshape(B, T, 4, KV_HEADS, HEAD_DIM)
        wrows = proj[:, d_pool + D_ATTN + 4 * D_KV:d_main].reshape(B, T, 2, KV_HEADS, HEAD_DIM)
        u = proj[:, :d_pool].reshape(B, T, d_pool)
        if past is None:
            proj3 = proj.reshape(B, T, d_main)
            pre16 = jnp.zeros((B, POOL_KEEP + 1, d_pool), f32)
            pool_out = pool_prompt(proj3, pre16, p["pool_w"], p["pool_scale"], l, t0).reshape(M, d_pool)
            kc = compress_prompt(proj3, wc, cb1, p["cmp_w2"], l, kv_col)
            attn = nsa_prompt(proj3, gn.reshape(B, T, LANES), kc, q_blk0 // (GROUP * HEAD_DIM // tn_main), kv_col)
            attn = attn.reshape(M, D_ATTN)
            win_new.append(wrows[:, -min(WINDOW, T):])
            pool_new.append(u[:, -POOL_KEEP:])
        else:
            state_t = jnp.transpose(state_pool[l], (1, 0, 2))
            pool_out = pool_sample(proj, state_t, p["pool_w"], p["pool_scale"], l, t0)
            hbuf = compress_pages(cache_pages, page_table, wc, l, n_pool)
            kc = compress_finish(hbuf, cb1, p["cmp_w2"], l)
            q3 = proj[:, d_pool:d_pool + D_ATTN].reshape(B, N_HEADS, HEAD_DIM)
            o_c, selmask = nsa_select(q3, kc, ov_s, t0, n_blk)
            cur = t0 // SEL_LEN
            n_gather = N_SEL - 1
            others = selmask[:, :, :n_blk].at[:, :, cur].set(0.0)
            _, blk_idx = lax.top_k(others, n_gather)
            per_page = PAGE_SIZE // SEL_LEN
            page = jnp.take_along_axis(page_table, blk_idx.reshape(B, -1) // per_page, axis=1)
            tok_off = ((l * n_pool + page) * PAGE_SIZE + (blk_idx.reshape(B, -1) % per_page) * SEL_LEN)
            new3 = proj[:, d_pool + D_ATTN:d_main].reshape(B, 6 * KV_HEADS, HEAD_DIM)
            attn = nsa_decode(tok_off.reshape(-1).astype(i32), cache_rows, q3, o_c, gn.reshape(B, 1, LANES),
                              new3, state_win4, l, n_gather).reshape(M, D_ATTN)
            win_new.append(jnp.concatenate([state_win[l], wrows], axis=1)[:, -WINDOW:])
            pool_new.append(jnp.concatenate([state_pool[l], u], axis=1)[:, -POOL_KEEP:])
        kv_new.append(rows)

        pp = _matmul(pool_out, [p["w_pool_proj"]], (l,), D, 0, [], _no_extras, _ep_plain, f32, tm_pref=tm, name="pool_proj")

        def merge_specs(tm_, tn_):
            nb = D // tn_
            return [pl.BlockSpec((tm_, tn_), lambda i, j, k: (i, j)),
                    pl.BlockSpec((tm_, tn_), lambda i, j, k: (i, j)),
                    pl.BlockSpec((tm_, tn_), lambda i, j, k: (i, j + nb))]

        h = _matmul(attn, [p["w_attn_proj"]], (l,), D, 0, [pp, g_mrg, g_mrg], merge_specs, _ep_merge, bf16,
                    tm_pref=tm, name="attn_proj_merge")

        def res_specs(tm_, tn_):
            return [pl.BlockSpec((tm_, tn_), lambda i, j, k: (i, j))]

        x2 = _matmul(h, [p["w_out"]], (l,), D, 0, [x2], res_specs, _ep_residual, f32, tm_pref=tm, name="out_proj")

        xn = rmsnorm(x2, p["norm_ffn"], l, bf16)
        if l % 2 == 0:
            hid = _matmul(xn, [p["ffn_wg"], p["ffn_wu"]], (l // 2,), p["ffn_wg"].shape[-1], 0, [], _no_extras,
                          _ep_swiglu, bf16, tm_pref=tm, name="ffn_up")
            x2 = _matmul(hid, [p["ffn_wd"]], (l // 2,), D, 0, [x2], res_specs, _ep_residual, f32,
                         tm_pref=tm, name="ffn_down")
        else:
            w_r = jnp.pad(p["moe_router"], ((0, 0), (0, 0), (0, LANES - N_EXPERTS)))
            comb = _matmul(xn, [w_r], (l // 2,), LANES, 0, [], _no_extras, _ep_router, f32, tm_pref=tm, name="router")
            out = None
            for e in range(N_EXPERTS):
                hid = _matmul(xn, [p["moe_wg"], p["moe_wu"]], (l // 2, e), p["moe_wg"].shape[-1], 0, [], _no_extras,
                              _ep_swiglu, bf16, tm_pref=tm, name="moe_up")
                first, last = e == 0, e == N_EXPERTS - 1
                extras = [comb] + ([] if first else [out]) + ([x2] if last else [])

                def moe_specs(tm_, tn_, n=len(extras)):
                    return ([pl.BlockSpec((tm_, LANES), lambda i, j, k: (i, 0))]
                            + [pl.BlockSpec((tm_, tn_), lambda i, j, k: (i, j))] * (n - 1))

                out = _matmul(hid, [p["moe_wd"]], (l // 2, e), D, 0, extras, moe_specs,
                              functools.partial(_ep_moe_acc, expert=e, first=first, last=last), f32,
                              tm_pref=tm, name="moe_down")
            x2 = out

    y = rmsnorm(x2, p["norm_final"][None], 0, f32).reshape(B, T, D)
    return y, jnp.stack(kv_new, 0), jnp.stack(win_new, 0), jnp.stack(pool_new, 0)


def kernel(x_prompt, x_sample, cache_kv, state_win, state_pool, page_table, norm_mix, w_in, pool_w, pool_scale,
           cmp_w1, cmp_b1, cmp_w2, w_pool_proj, w_attn_proj, w_out, norm_ffn, ffn_wg, ffn_wu, ffn_wd,
           moe_router, moe_wg, moe_wu, moe_wd, norm_final):
    p = dict(norm_mix=norm_mix, w_in=w_in, pool_w=pool_w, pool_scale=pool_scale, cmp_w1=cmp_w1, cmp_b1=cmp_b1,
             cmp_w2=cmp_w2, w_pool_proj=w_pool_proj, w_attn_proj=w_attn_proj, w_out=w_out, norm_ffn=norm_ffn,
             ffn_wg=ffn_wg, ffn_wu=ffn_wu, ffn_wd=ffn_wd, moe_router=moe_router, moe_wg=moe_wg, moe_wu=moe_wu,
             moe_wd=moe_wd, norm_final=norm_final)
    y_p, kv_p, win_p, pool_p = _trunk(x_prompt, 0, None, p)
    past_len = page_table.shape[1] * PAGE_SIZE
    y_s, kv_s, win_s, pool_s = _trunk(x_sample, past_len, (cache_kv, page_table, state_win, state_pool), p)
    return (y_p, y_s, kv_p, kv_s, win_p, win_s, pool_p, pool_s)
```

```python
import functools
import math

import jax
import jax.numpy as jnp
from jax import lax
from jax.experimental import pallas as pl
from jax.experimental.pallas import tpu as pltpu

f32 = jnp.float32
bf16 = jnp.bfloat16
i32 = jnp.int32

N_HEADS = 16
HEAD_DIM = 128
KV_HEADS = 4
GROUP = N_HEADS // KV_HEADS
CMP_LEN = 32
CMP_STRIDE = 16
SEL_LEN = 64
N_SEL = 16
WINDOW = 512
N_NSA_BRANCH = 3
FORCE = 1e4
ROPE_THETA = 10000.0
POOL_WINDOWS = (2, 4, 8, 16)
POOL_KEEP = max(POOL_WINDOWS) - 1
N_EXPERTS = 8
EPS = 1e-6
NEG = -1e30
SCALE = HEAD_DIM ** -0.5
PAGE_SIZE = 128
LANES = 128
D_KV = KV_HEADS * HEAD_DIM
D_ATTN = N_HEADS * HEAD_DIM
VMEM_LIMIT = 56 * 1024 * 1024

_NT = (((1,), (1,)), ((), ()))


def _cparams(sem):
    return pltpu.CompilerParams(dimension_semantics=sem, vmem_limit_bytes=VMEM_LIMIT)


def _pick(dim, pref, mult=LANES):
    if dim <= pref:
        return dim
    t = (pref // mult) * mult
    while t >= mult:
        if dim % t == 0:
            return t
        t -= mult
    return dim


def _sigmoid(x):
    return 1.0 / (1.0 + jnp.exp(-x))


def _gelu_tanh(x):
    c = math.sqrt(2.0 / math.pi)
    return x * (0.5 * (1.0 + jnp.tanh(c * (x + 0.044715 * (x * x * x)))))


def _rmsnorm_kernel(x_ref, g_ref, o_ref):
    x = x_ref[...]
    ms = jnp.mean(x * x, axis=-1, keepdims=True)
    o_ref[...] = ((x * lax.rsqrt(ms + EPS)) * g_ref[...]).astype(o_ref.dtype)


def rmsnorm(x, gains, layer, out_dtype):
    M, D = x.shape
    tm = _pick(M, 512, 8)
    g3 = gains.reshape(gains.shape[0], 1, D)
    return pl.pallas_call(
        _rmsnorm_kernel,
        grid=(M // tm,),
        in_specs=[pl.BlockSpec((tm, D), lambda i: (i, 0)),
                  pl.BlockSpec((None, 1, D), lambda i: (layer, 0, 0))],
        out_specs=pl.BlockSpec((tm, D), lambda i: (i, 0)),
        out_shape=jax.ShapeDtypeStruct((M, D), out_dtype),
        compiler_params=_cparams(("parallel",)),
        name="rmsnorm",
    )(x, g3)


def _split(a):
    hi = a.astype(bf16)
    return hi, (a - hi.astype(f32)).astype(bf16)


def _dot_p(a, b, dims=None):
    dims = dims or (((a.ndim - 1,), (0,)), ((), ()))
    a_parts = _split(a) if a.dtype == f32 else (a,)
    b_parts = _split(b) if b.dtype == f32 else (b,)
    out = None
    for ia, ap in enumerate(a_parts):
        for ib, bp in enumerate(b_parts):
            if ia + ib < 2:
                t = lax.dot_general(ap, bp, dims, preferred_element_type=f32)
                out = t if out is None else out + t
    return out


def _mm_body(x_ref, w_refs, extras, o_ref, acc_refs, nk, epilogue, precise):
    j = pl.program_id(1)
    k = pl.program_id(2)
    if precise:
        parts = [_dot_p(x_ref[...], w_ref[...]) for w_ref in w_refs]
    else:
        x = x_ref[...].astype(bf16)
        parts = [jnp.dot(x, w_ref[...].astype(bf16), preferred_element_type=f32) for w_ref in w_refs]
    if nk == 1:
        epilogue(parts, extras, o_ref, j)
        return

    @pl.when(k == 0)
    def _():
        for a, p in zip(acc_refs, parts):
            a[...] = p

    @pl.when(k > 0)
    def _():
        for a, p in zip(acc_refs, parts):
            a[...] += p

    @pl.when(k == nk - 1)
    def _():
        epilogue([a[...] for a in acc_refs], extras, o_ref, j)


def _mm_kernel(*refs, n_w, n_extra, nk, epilogue, grouped, precise):
    if grouped:
        _, n_used_ref = refs[:2]
        refs = refs[2:]
    x_ref = refs[0]
    w_refs = refs[1:1 + n_w]
    extras = refs[1 + n_w:1 + n_w + n_extra]
    o_ref = refs[1 + n_w + n_extra]
    acc_refs = refs[2 + n_w + n_extra:]
    if not grouped:
        _mm_body(x_ref, w_refs, extras, o_ref, acc_refs, nk, epilogue, precise)
        return
    used = pl.program_id(0) < n_used_ref[0]

    @pl.when(used)
    def _():
        _mm_body(x_ref, w_refs, extras, o_ref, acc_refs, nk, epilogue, precise)

    @pl.when(jnp.logical_not(used))
    def _():
        o_ref[...] = jnp.zeros(o_ref.shape, o_ref.dtype)


def _matmul(x, ws, w_lead, n_cols, col_off_blocks, extras, extra_specs, epilogue, out_dtype,
            tm_pref=1024, tn_pref=512, tk_pref=2048, name="matmul", group=None, precise=False):
    M, K = x.shape
    tm = _pick(M, tm_pref, 8)
    tn = _pick(n_cols, tn_pref)
    tk = _pick(K, tk_pref)
    nk = K // tk
    grouped = group is not None
    n_lead = len(w_lead) + (1 if grouped else 0)

    def w_index(i, j, k, *pref):
        lead = tuple(w_lead) + ((pref[0][i],) if grouped else ())
        return lead + (k, j + col_off_blocks)

    w_spec = pl.BlockSpec((None,) * n_lead + (tk, tn), w_index)
    kern = functools.partial(_mm_kernel, n_w=len(ws), n_extra=len(extras), nk=nk, epilogue=epilogue,
                             grouped=grouped, precise=precise)
    scratch = [pltpu.VMEM((tm, tn), f32) for _ in ws] if nk > 1 else []
    grid_spec = pltpu.PrefetchScalarGridSpec(
        num_scalar_prefetch=2 if grouped else 0,
        grid=(M // tm, n_cols // tn, nk),
        in_specs=[pl.BlockSpec((tm, tk), lambda i, j, k, *_: (i, k))] + [w_spec] * len(ws) + list(extra_specs(tm, tn)),
        out_specs=pl.BlockSpec((tm, tn), lambda i, j, k, *_: (i, j)),
        scratch_shapes=scratch,
    )
    return pl.pallas_call(
        kern,
        grid_spec=grid_spec,
        out_shape=jax.ShapeDtypeStruct((M, n_cols), out_dtype),
        compiler_params=_cparams(("parallel", "parallel", "arbitrary")),
        name=name,
    )(*(group or ()), x, *ws, *extras)


def _no_extras(tm, tn):
    return []


def _ep_plain(accs, extras, o_ref, j):
    o_ref[...] = accs[0].astype(o_ref.dtype)


def _ep_residual(accs, extras, o_ref, j):
    o_ref[...] = (extras[0][...] + accs[0]).astype(o_ref.dtype)


def _ep_swiglu(accs, extras, o_ref, j):
    g, u = accs
    o_ref[...] = ((g * _sigmoid(g)) * u).astype(o_ref.dtype)


def _ep_merge(accs, extras, o_ref, j):
    p_ref, g0_ref, g1_ref = extras
    h = _sigmoid(g0_ref[...]) * p_ref[...] + _sigmoid(g1_ref[...]) * accs[0]
    o_ref[...] = h.astype(o_ref.dtype)


def _rope_tile(v, cos, sin):
    outs = []
    for h in range(v.shape[1] // HEAD_DIM):
        seg = v[:, h * HEAD_DIM:(h + 1) * HEAD_DIM]
        outs.append(seg * cos + pltpu.roll(seg, HEAD_DIM // 2, axis=1) * sin)
    return jnp.concatenate(outs, axis=1) if len(outs) > 1 else outs[0]


def _ep_inproj(accs, extras, o_ref, j, *, rope_blocks):
    cos_ref, sin_ref = extras
    flag = functools.reduce(jnp.logical_or, [j == b for b in rope_blocks])

    @pl.when(flag)
    def _():
        o_ref[...] = _rope_tile(accs[0], cos_ref[...], sin_ref[...]).astype(o_ref.dtype)

    @pl.when(jnp.logical_not(flag))
    def _():
        o_ref[...] = accs[0].astype(o_ref.dtype)


def _ep_router(accs, extras, o_ref, j):
    logits = accs[0]
    lane = lax.broadcasted_iota(i32, logits.shape, 1)
    lg = jnp.where(lane < N_EXPERTS, logits, -jnp.inf)
    m1 = jnp.max(lg, axis=-1, keepdims=True)
    i1 = jnp.min(jnp.where(lg == m1, lane, LANES), axis=-1, keepdims=True)
    lg2 = jnp.where(lane == i1, -jnp.inf, lg)
    m2 = jnp.max(lg2, axis=-1, keepdims=True)
    i2 = jnp.min(jnp.where(lg2 == m2, lane, LANES), axis=-1, keepdims=True)
    e2 = jnp.exp(m2 - m1)
    den = 1.0 + e2
    comb = jnp.where(lane == i1, 1.0 / den, 0.0) + jnp.where(lane == i2, e2 / den, 0.0)
    o_ref[...] = comb.astype(o_ref.dtype)


def _ep_moe_acc(accs, extras, o_ref, j, *, expert, first, last):
    lane_src = extras[0][...]
    lane = lax.broadcasted_iota(i32, lane_src.shape, 1)
    c = jnp.sum(jnp.where(lane == expert, lane_src, 0.0), axis=-1, keepdims=True)
    out = c * accs[0]
    nxt = 1
    if not first:
        out = extras[nxt][...] + out
        nxt += 1
    if last:
        out = extras[nxt][...] + out
    o_ref[...] = out.astype(o_ref.dtype)


def _route(comb, tm):
    M = comb.shape[0]
    gates, idx = lax.top_k(comb[:, :N_EXPERTS], 2)
    e_flat = idx.reshape(-1).astype(i32)
    onehot = (e_flat[:, None] == jnp.arange(N_EXPERTS, dtype=i32)[None, :]).astype(i32)
    csum = jnp.cumsum(onehot, axis=0)
    rank = jnp.sum((csum - onehot) * onehot, axis=1)
    tiles_e = (csum[-1] + tm - 1) // tm
    tile_end = jnp.cumsum(tiles_e)
    pos = (tile_end - tiles_e)[e_flat] * tm + rank
    n_tiles = 2 * M // tm + N_EXPERTS
    src_tok = jnp.zeros((n_tiles * tm,), i32).at[pos].set(jnp.arange(2 * M, dtype=i32) // 2)
    n_used = tile_end[-1]
    t = jnp.minimum(jnp.arange(n_tiles, dtype=i32), n_used - 1)
    tile_expert = jnp.minimum(jnp.searchsorted(tile_end, t, side="right"), N_EXPERTS - 1).astype(i32)
    return gates, pos.reshape(M, 2).astype(i32), src_tok, tile_expert, n_used.reshape(1).astype(i32)


def _row_copy(src_hbm, row, dst, r, sem):
    return pltpu.make_async_copy(src_hbm.at[pl.ds(row, 1), :], dst.at[pl.ds(r, 1), :], sem)


def _dispatch_kernel(src_ref, x_ref, o_ref, buf, sem, *, rows):
    base = pl.program_id(0) * rows

    def start(r, c):
        _row_copy(x_ref, src_ref[base + r], buf, r, sem.at[0]).start()
        return c

    def wait(r, c):
        _row_copy(x_ref, src_ref[base + r], buf, r, sem.at[0]).wait()
        return c

    lax.fori_loop(0, rows, start, 0, unroll=8)
    lax.fori_loop(0, rows, wait, 0, unroll=8)
    o_ref[...] = buf[...].astype(o_ref.dtype)


def moe_dispatch(x, src_tok):
    P = src_tok.shape[0]
    D = x.shape[1]
    rows = _pick(P, 256, 8)
    grid_spec = pltpu.PrefetchScalarGridSpec(
        num_scalar_prefetch=1,
        grid=(P // rows,),
        in_specs=[pl.BlockSpec(memory_space=pl.ANY)],
        out_specs=pl.BlockSpec((rows, D), lambda i, s: (i, 0)),
        scratch_shapes=[pltpu.VMEM((rows, D), f32), pltpu.SemaphoreType.DMA((1,))],
    )
    return pl.pallas_call(
        functools.partial(_dispatch_kernel, rows=rows),
        grid_spec=grid_spec,
        out_shape=jax.ShapeDtypeStruct((P, D), bf16),
        compiler_params=_cparams(("arbitrary",)),
        name="moe_dispatch",
    )(src_tok, x)


def _combine_kernel(pos_ref, y_ref, x_ref, g_ref, o_ref, buf, sem, *, rows):
    base = pl.program_id(0) * rows

    def start(r, c):
        for j in range(2):
            _row_copy(y_ref, pos_ref[2 * (base + r) + j], buf.at[j], r, sem.at[j]).start()
        return c

    def wait(r, c):
        for j in range(2):
            _row_copy(y_ref, pos_ref[2 * (base + r) + j], buf.at[j], r, sem.at[j]).wait()
        return c

    lax.fori_loop(0, rows, start, 0, unroll=4)
    lax.fori_loop(0, rows, wait, 0, unroll=4)
    g = g_ref[...]
    lane = lax.broadcasted_iota(i32, g.shape, 1)
    g0 = jnp.sum(jnp.where(lane == 0, g, 0.0), axis=-1, keepdims=True)
    g1 = jnp.sum(jnp.where(lane == 1, g, 0.0), axis=-1, keepdims=True)
    o_ref[...] = x_ref[...] + (g0 * buf[0] + g1 * buf[1])


def moe_combine(y_sorted, pos, gates_pad, x):
    M, D = x.shape
    rows = _pick(M, 256, 8)
    grid_spec = pltpu.PrefetchScalarGridSpec(
        num_scalar_prefetch=1,
        grid=(M // rows,),
        in_specs=[pl.BlockSpec(memory_space=pl.ANY),
                  pl.BlockSpec((rows, D), lambda i, s: (i, 0)),
                  pl.BlockSpec((rows, LANES), lambda i, s: (i, 0))],
        out_specs=pl.BlockSpec((rows, D), lambda i, s: (i, 0)),
        scratch_shapes=[pltpu.VMEM((2, rows, D), f32), pltpu.SemaphoreType.DMA((2,))],
    )
    return pl.pallas_call(
        functools.partial(_combine_kernel, rows=rows),
        grid_spec=grid_spec,
        out_shape=jax.ShapeDtypeStruct((M, D), f32),
        compiler_params=_cparams(("arbitrary",)),
        name="moe_combine",
    )(pos.reshape(-1), y_sorted, x, gates_pad)


def _pool_groups(shifted, u, pos1, pw_ref, sc_ref, o_ref, precise=False):
    pg = u.shape[1] // len(POOL_WINDOWS)
    for g, w in enumerate(POOL_WINDOWS):
        c = slice(g * pg, (g + 1) * pg)
        s = u[:, c]
        for jj in range(1, w):
            s = s + shifted(jj, c)
        cnt = jnp.minimum(float(w), pos1)
        pooled = s / cnt - u[:, c]
        if precise:
            mixed = _dot_p(pooled, pw_ref[g])
        else:
            mixed = jnp.dot(pooled.astype(bf16), pw_ref[g].astype(bf16), preferred_element_type=f32)
        o_ref[:, c] = (mixed * sc_ref[:, c]).astype(o_ref.dtype)


def _pool_prompt_kernel(u_ref, pre_ref, pw_ref, sc_ref, o_ref, ext_ref, *, tp, t0):
    i = pl.program_id(1)
    halo = POOL_KEEP + 1

    @pl.when(i == 0)
    def _():
        ext_ref[0:halo, :] = pre_ref[...]

    u = u_ref[...]
    ext_ref[halo:halo + tp, :] = u
    pos1 = (lax.broadcasted_iota(i32, (tp, 1), 0) + (i * tp + t0 + 1)).astype(f32)
    _pool_groups(lambda jj, c: ext_ref[halo - jj:halo - jj + tp, c], u, pos1, pw_ref, sc_ref, o_ref)
    ext_ref[0:halo, :] = ext_ref[tp:tp + halo, :]


def pool_prompt(proj3, prefix16, pool_w, pool_scale, layer, t0):
    B, T, _ = proj3.shape
    dp = pool_scale.shape[-1]
    tp = _pick(T, 512, 16)
    kern = functools.partial(_pool_prompt_kernel, tp=tp, t0=t0)
    sc3 = pool_scale.reshape(pool_scale.shape[0], 1, dp)
    return pl.pallas_call(
        kern,
        grid=(B, T // tp),
        in_specs=[pl.BlockSpec((None, tp, dp), lambda b, i: (b, i, 0)),
                  pl.BlockSpec((None, POOL_KEEP + 1, dp), lambda b, i: (b, 0, 0)),
                  pl.BlockSpec((None,) + pool_w.shape[1:], lambda b, i: (layer, 0, 0, 0)),
                  pl.BlockSpec((None, 1, dp), lambda b, i: (layer, 0, 0))],
        out_specs=pl.BlockSpec((None, tp, dp), lambda b, i: (b, i, 0)),
        out_shape=jax.ShapeDtypeStruct((B, T, dp), bf16),
        scratch_shapes=[pltpu.VMEM((tp + POOL_KEEP + 1, dp), f32)],
        compiler_params=_cparams(("arbitrary", "arbitrary")),
        name="pool_prompt",
    )(proj3, prefix16, pool_w, sc3)


def _pool_sample_kernel(u_ref, st_ref, pw_ref, sc_ref, o_ref, *, t0):
    u = u_ref[...]
    pos1 = jnp.full((u.shape[0], 1), float(t0 + 1), f32)
    _pool_groups(lambda jj, c: st_ref[POOL_KEEP - jj, :, c], u, pos1, pw_ref, sc_ref, o_ref, precise=True)


def pool_sample(proj, state_t, pool_w, pool_scale, layer, t0):
    B = proj.shape[0]
    dp = pool_scale.shape[-1]
    kern = functools.partial(_pool_sample_kernel, t0=t0)
    sc3 = pool_scale.reshape(pool_scale.shape[0], 1, dp)
    return pl.pallas_call(
        kern,
        grid=(1,),
        in_specs=[pl.BlockSpec((B, dp), lambda i: (0, 0)),
                  pl.BlockSpec((POOL_KEEP, B, dp), lambda i: (0, 0, 0)),
                  pl.BlockSpec((None,) + pool_w.shape[1:], lambda i: (layer, 0, 0, 0)),
                  pl.BlockSpec((None, 1, dp), lambda i: (layer, 0, 0))],
        out_specs=pl.BlockSpec((B, dp), lambda i: (0, 0)),
        out_shape=jax.ShapeDtypeStruct((B, dp), f32),
        compiler_params=_cparams(("arbitrary",)),
        name="pool_sample",
    )(proj, state_t, pool_w, sc3)


def _compress_tail(h01, b1, w2):
    n = h01.shape[0]
    hd = h01.shape[1] // 2
    hid = h01[:, :hd] + pltpu.roll(h01[:, hd:], n - 1, axis=0) + b1
    return jnp.dot(_gelu_tanh(hid).astype(bf16), w2.astype(bf16), preferred_element_type=f32)


def _compress_prompt_kernel(x_ref, wc_ref, b1_ref, w2_ref, o_ref, *, n_sub):
    acc = None
    for l in range(CMP_STRIDE):
        xl = x_ref[pl.ds(l, n_sub, stride=CMP_STRIDE), :].astype(bf16)
        p = jnp.dot(xl, wc_ref[l].astype(bf16), preferred_element_type=f32)
        acc = p if acc is None else acc + p
    o_ref[...] = _compress_tail(acc, b1_ref[...], w2_ref[...]).astype(o_ref.dtype)


def compress_prompt(proj3, wc, b1, w2, layer, kv_col_block):
    B, T, _ = proj3.shape
    n_sub = T // CMP_STRIDE
    hid2 = wc.shape[-1]
    kern = functools.partial(_compress_prompt_kernel, n_sub=n_sub)
    return pl.pallas_call(
        kern,
        grid=(B, 2, KV_HEADS),
        in_specs=[pl.BlockSpec((None, T, HEAD_DIM), lambda b, r, h: (b, 0, kv_col_block + r * KV_HEADS + h)),
                  pl.BlockSpec((None, None, CMP_STRIDE, HEAD_DIM, hid2), lambda b, r, h: (layer, r, 0, 0, 0)),
                  pl.BlockSpec((None, None, 1, hid2 // 2), lambda b, r, h: (layer, r, 0, 0)),
                  pl.BlockSpec((None, None, hid2 // 2, HEAD_DIM), lambda b, r, h: (layer, r, 0, 0))],
        out_specs=pl.BlockSpec((None, None, None, n_sub, HEAD_DIM), lambda b, r, h: (b, r, h, 0, 0)),
        out_shape=jax.ShapeDtypeStruct((B, 2, KV_HEADS, n_sub, HEAD_DIM), bf16),
        compiler_params=_cparams(("parallel", "parallel", "parallel")),
        name="compress_prompt",
    )(proj3, wc, b1, w2)


def _compress_pages_kernel(pt_ref, *refs, pg):
    page_refs = refs[:pg]
    wc_ref = refs[pg]
    o_ref = refs[pg + 1]
    sub = PAGE_SIZE // CMP_STRIDE
    for r in range(2):
        acc = None
        for l in range(CMP_STRIDE):
            rows = [page_refs[s][pl.ds(l, sub, stride=CMP_STRIDE), r * KV_HEADS + h, :]
                    for h in range(KV_HEADS) for s in range(pg)]
            xl = jnp.concatenate(rows, axis=0).astype(bf16)
            p = jnp.dot(xl, wc_ref[r, l].astype(bf16), preferred_element_type=f32)
            acc = p if acc is None else acc + p
        for h in range(KV_HEADS):
            o_ref[r, h] = acc[h * pg * sub:(h + 1) * pg * sub]


def compress_pages(cache_pages, page_table, wc, layer, n_pool):
    B, n_pages = page_table.shape
    pg = math.gcd(n_pages, 8)
    sub = PAGE_SIZE // CMP_STRIDE
    hid2 = wc.shape[-1]
    kern = functools.partial(_compress_pages_kernel, pg=pg)

    def page_spec(s):
        return pl.BlockSpec((None, PAGE_SIZE, 2 * KV_HEADS, HEAD_DIM),
                            lambda b, p, pt: (layer * n_pool + pt[b, p * pg + s], 0, 0, 0))

    grid_spec = pltpu.PrefetchScalarGridSpec(
        num_scalar_prefetch=1,
        grid=(B, n_pages // pg),
        in_specs=[page_spec(s) for s in range(pg)]
        + [pl.BlockSpec((None, 2, CMP_STRIDE, HEAD_DIM, hid2), lambda b, p, pt: (layer, 0, 0, 0, 0))],
        out_specs=pl.BlockSpec((None, 2, KV_HEADS, pg * sub, hid2), lambda b, p, pt: (b, 0, 0, p, 0)),
    )
    return pl.pallas_call(
        kern,
        grid_spec=grid_spec,
        out_shape=jax.ShapeDtypeStruct((B, 2, KV_HEADS, n_pages * sub, hid2), f32),
        compiler_params=_cparams(("parallel", "parallel")),
        name="compress_pages",
    )(page_table, *([cache_pages] * pg), wc)


def _compress_finish_kernel(h_ref, b1_ref, w2_ref, o_ref):
    for h in range(KV_HEADS):
        o_ref[h] = _compress_tail(h_ref[h], b1_ref[...], w2_ref[...]).astype(o_ref.dtype)


def compress_finish(hbuf, b1, w2, layer):
    B, _, _, n_sub, hid2 = hbuf.shape
    return pl.pallas_call(
        _compress_finish_kernel,
        grid=(B, 2),
        in_specs=[pl.BlockSpec((None, None, KV_HEADS, n_sub, hid2), lambda b, r: (b, r, 0, 0, 0)),
                  pl.BlockSpec((None, None, 1, hid2 // 2), lambda b, r: (layer, r, 0, 0)),
                  pl.BlockSpec((None, None, hid2 // 2, HEAD_DIM), lambda b, r: (layer, r, 0, 0))],
        out_specs=pl.BlockSpec((None, None, KV_HEADS, n_sub, HEAD_DIM), lambda b, r: (b, r, 0, 0, 0)),
        out_shape=jax.ShapeDtypeStruct((B, 2, KV_HEADS, n_sub, HEAD_DIM), bf16),
        compiler_params=_cparams(("parallel", "parallel")),
        name="compress_finish",
    )(hbuf, b1, w2)


def _masked_softmax(s, mask, axis):
    s = jnp.where(mask, s, NEG)
    m = jnp.max(s, axis=axis, keepdims=True)
    e = jnp.where(mask, jnp.exp(s - m), 0.0)
    l = jnp.sum(e, axis=axis, keepdims=True)
    return e * jnp.where(l > 0.0, 1.0 / l, 0.0)


def _masked_softmax_rows(s, mask):
    return _masked_softmax(s, mask, -1)


def _nsa_prompt_kernel(q_ref, gn_ref, kc_ref, vc_ref, ks_ref, vs_ref, kw_ref, vw_ref, ovt_ref, ext_ref,
                       o_ref, ksb, vst, kwb, vwt, vct, m_sc, l_sc, acc_sc, *, tq, n_blk, kts, wlen, seq):
    kh = pl.program_id(1)
    i = pl.program_id(2)
    t0 = i * tq
    cols = GROUP * tq

    @pl.when(i == 0)
    def _():
        ch = min(seq, 512)
        for c in range(seq // ch):
            sl = slice(c * ch, (c + 1) * ch)
            ksb[sl, :] = ks_ref[sl, :].astype(bf16)
            kwb[sl, :] = kw_ref[sl, :].astype(bf16)
            vst[:, sl] = vs_ref[sl, :].T.astype(bf16)
            vwt[:, sl] = vw_ref[sl, :].T.astype(bf16)
        vct[...] = vc_ref[...].astype(f32).T.astype(bf16)

    q_all = jnp.concatenate([q_ref[:, g * HEAD_DIM:(g + 1) * HEAD_DIM] for g in range(GROUP)], axis=0).astype(bf16)
    t_q = t0 + lax.broadcasted_iota(i32, (1, tq), 1)
    t_col = jnp.concatenate([t_q] * GROUP, axis=1)

    n_cp = kc_ref.shape[0]
    s = lax.dot_general(kc_ref[...], q_all, _NT, preferred_element_type=f32) * SCALE
    cmp_end = lax.broadcasted_iota(i32, (n_cp, 1), 0) * CMP_STRIDE + (CMP_LEN - 1)
    p_c = _masked_softmax(s, cmp_end <= t_col, 0).astype(bf16)
    o_c = jnp.dot(vct[...], p_c, preferred_element_type=f32)

    imp_g = jnp.dot(ovt_ref[...], p_c, preferred_element_type=f32)
    imp = imp_g[:, 0:tq]
    for g in range(1, GROUP):
        imp = imp + imp_g[:, g * tq:(g + 1) * tq]
    nbr = imp.shape[0]
    blk = lax.broadcasted_iota(i32, (nbr, 1), 0)
    cur = t_q // SEL_LEN
    valid = blk * SEL_LEN <= t_q
    forced = (blk == 0) | (blk == cur) | (blk == cur - 1)
    imp = jnp.where(valid, jnp.where(forced, FORCE, imp), -jnp.inf)
    rank = jnp.zeros(imp.shape, f32)
    for ii in range(n_blk):
        row = imp[ii:ii + 1, :]
        beats = (row > imp) | ((row == imp) & (blk > ii))
        rank = rank + jnp.where(beats, 1.0, 0.0)
    sel = jnp.where((rank < float(N_SEL)) & valid, 1.0, 0.0).astype(bf16)

    m_sc[...] = jnp.full(m_sc.shape, NEG, f32)
    l_sc[...] = jnp.zeros(l_sc.shape, f32)
    acc_sc[...] = jnp.zeros(acc_sc.shape, f32)

    def body(kt, carry):
        off = pl.multiple_of(kt * kts, kts)
        sc = lax.dot_general(ksb[pl.ds(off, kts), :], q_all, _NT, preferred_element_type=f32) * SCALE
        hit = jnp.dot(ext_ref[kt], sel, preferred_element_type=f32)
        kpos = off + lax.broadcasted_iota(i32, (kts, 1), 0)
        ok = (jnp.concatenate([hit] * GROUP, axis=1) > 0.5) & (kpos <= t_col)
        sc = jnp.where(ok, sc, NEG)
        m_old = m_sc[...]
        m_new = jnp.maximum(m_old, jnp.max(sc, axis=0, keepdims=True))
        alpha = jnp.exp(m_old - m_new)
        e = jnp.where(ok, jnp.exp(sc - m_new), 0.0)
        l_sc[...] = alpha * l_sc[...] + jnp.sum(e, axis=0, keepdims=True)
        acc_sc[...] = alpha * acc_sc[...] + jnp.dot(vst[:, pl.ds(off, kts)], e.astype(bf16),
                                                    preferred_element_type=f32)
        m_sc[...] = m_new
        return carry

    lax.fori_loop(0, (t0 + tq - 1) // kts + 1, body, 0)
    l = l_sc[...]
    o_s = acc_sc[...] * jnp.where(l > 0.0, 1.0 / l, 0.0)

    woff = pl.multiple_of(jnp.clip(t0 + tq - wlen, 0, seq - wlen), tq)
    sc = lax.dot_general(kwb[pl.ds(woff, wlen), :], q_all, _NT, preferred_element_type=f32) * SCALE
    kpos = woff + lax.broadcasted_iota(i32, (wlen, 1), 0)
    p_w = _masked_softmax(sc, (kpos <= t_col) & (kpos > t_col - WINDOW), 0).astype(bf16)
    o_w = jnp.dot(vwt[:, pl.ds(woff, wlen)], p_w, preferred_element_type=f32)

    sig_t = _sigmoid(gn_ref[...]).T
    gcol = lax.broadcasted_iota(i32, sig_t.shape, 0)
    for g in range(GROUP):
        c = slice(g * tq, (g + 1) * tq)
        out = None
        for br, o_br in enumerate((o_c, o_s, o_w)):
            colid = kh * (GROUP * N_NSA_BRANCH) + g * N_NSA_BRANCH + br
            gate = jnp.sum(jnp.where(gcol == colid, sig_t, 0.0), axis=0, keepdims=True)
            term = gate * o_br[:, c]
            out = term if out is None else out + term
        o_ref[:, g * HEAD_DIM:(g + 1) * HEAD_DIM] = out.T.astype(o_ref.dtype)


def _cmp_to_sel(n_rows, n_blk, n_cols):
    i0 = jnp.arange(n_rows)[:, None] * CMP_STRIDE
    j0 = jnp.arange(n_cols)[None, :] * SEL_LEN
    ov = jnp.clip(jnp.minimum(i0 + CMP_LEN, j0 + SEL_LEN) - jnp.maximum(i0, j0), 0, None)
    ov = jnp.where(jnp.arange(n_cols)[None, :] < n_blk, ov, 0)
    return (ov.astype(f32) / CMP_LEN).astype(bf16)


def nsa_prompt(proj3, gn3, kc, q_col, kv_col):
    B, T, _ = proj3.shape
    assert T % SEL_LEN == 0
    tq = _pick(T, 128, 64)
    kts = _pick(T, 512, tq)
    wlen = min(WINDOW + tq, T)
    assert T % tq == 0 and WINDOW % tq == 0
    n_blk = T // SEL_LEN
    nbr = -(-n_blk // SEL_LEN) * SEL_LEN
    n_cp = T // CMP_STRIDE
    ovt = _cmp_to_sel(n_cp, n_blk, nbr).T
    tok_blk = jnp.arange(T) // SEL_LEN
    ext = (tok_blk[:, None] == jnp.arange(nbr)[None, :]).astype(bf16).reshape(T // kts, kts, nbr)
    cols = GROUP * tq
    kern = functools.partial(_nsa_prompt_kernel, tq=tq, n_blk=n_blk, kts=kts, wlen=wlen, seq=T)

    def kv_spec(r):
        return pl.BlockSpec((None, T, HEAD_DIM), lambda b, k, i: (b, 0, kv_col + r * KV_HEADS + k))

    def cmp_spec(r):
        return pl.BlockSpec((None, None, None, n_cp, HEAD_DIM), lambda b, k, i: (b, r, k, 0, 0))

    return pl.pallas_call(
        kern,
        grid=(B, KV_HEADS, T // tq),
        in_specs=[pl.BlockSpec((None, tq, GROUP * HEAD_DIM), lambda b, k, i: (b, i, q_col + k)),
                  pl.BlockSpec((None, tq, LANES), lambda b, k, i: (b, i, 0)),
                  cmp_spec(0), cmp_spec(1), kv_spec(2), kv_spec(3), kv_spec(4), kv_spec(5),
                  pl.BlockSpec((nbr, n_cp), lambda b, k, i: (0, 0)),
                  pl.BlockSpec((T // kts, kts, nbr), lambda b, k, i: (0, 0, 0))],
        out_specs=pl.BlockSpec((None, tq, GROUP * HEAD_DIM), lambda b, k, i: (b, i, k)),
        out_shape=jax.ShapeDtypeStruct((B, T, D_ATTN), bf16),
        scratch_shapes=[pltpu.VMEM((T, HEAD_DIM), bf16), pltpu.VMEM((HEAD_DIM, T), bf16),
                        pltpu.VMEM((T, HEAD_DIM), bf16), pltpu.VMEM((HEAD_DIM, T), bf16),
                        pltpu.VMEM((HEAD_DIM, n_cp), bf16),
                        pltpu.VMEM((1, cols), f32), pltpu.VMEM((1, cols), f32), pltpu.VMEM((HEAD_DIM, cols), f32)],
        compiler_params=_cparams(("parallel", "parallel", "arbitrary")),
        name="nsa_prompt",
    )(proj3, gn3, kc, kc, proj3, proj3, proj3, proj3, ovt, ext)


def _group_rows(n_rows):
    return lax.broadcasted_iota(i32, (n_rows, 1), 0) // GROUP


def _nsa_select_kernel(q_ref, kc_ref, vc_ref, ov_ref, oc_ref, sel_ref, *, t_pos, n_blk):
    q = q_ref[...]
    n_cp = kc_ref.shape[1]
    nbp = ov_ref.shape[1]
    grp = _group_rows(N_HEADS)
    cmp_end = lax.broadcasted_iota(i32, (1, n_cp), 1) * CMP_STRIDE + (CMP_LEN - 1)
    blk = lax.broadcasted_iota(i32, (1, nbp), 1)
    cur = t_pos // SEL_LEN
    valid = (blk * SEL_LEN <= t_pos) & (blk < n_blk)
    forced = (blk == 0) | (blk == cur) | (blk == cur - 1)
    ri = lax.broadcasted_iota(i32, (nbp, nbp), 0)
    ci = lax.broadcasted_iota(i32, (nbp, nbp), 1)
    o_c = jnp.zeros((N_HEADS, HEAD_DIM), f32)
    for k in range(KV_HEADS):
        s = _dot_p(q, kc_ref[k], _NT) * SCALE
        p = _masked_softmax_rows(s, cmp_end <= t_pos)
        o_k = _dot_p(p, vc_ref[k])
        o_c = o_c + jnp.where(grp == k, o_k, 0.0)
        imp_h = _dot_p(p, ov_ref[...])
        imp = jnp.sum(jnp.where(grp == k, imp_h, 0.0), axis=0, keepdims=True)
        imp = jnp.where(valid, jnp.where(forced, FORCE, imp), -jnp.inf)
        a = jnp.broadcast_to(imp, (nbp, nbp))
        at = jnp.broadcast_to(jnp.sum(jnp.where(ri == ci, a, 0.0), axis=1, keepdims=True), (nbp, nbp))
        beats = (at > a) | ((at == a) & (ri < ci))
        rank = jnp.sum(jnp.where(beats, 1.0, 0.0), axis=0, keepdims=True)
        sel_ref[k:k + 1, :] = jnp.where((rank < float(N_SEL)) & valid, 1.0, 0.0)
    oc_ref[...] = o_c


def nsa_select(q3, kc, ov, t_pos, n_blk):
    B = q3.shape[0]
    n_cp = kc.shape[3]
    nbp = ov.shape[1]
    kern = functools.partial(_nsa_select_kernel, t_pos=t_pos, n_blk=n_blk)
    return pl.pallas_call(
        kern,
        grid=(B,),
        in_specs=[pl.BlockSpec((None, N_HEADS, HEAD_DIM), lambda b: (b, 0, 0)),
                  pl.BlockSpec((None, None, KV_HEADS, n_cp, HEAD_DIM), lambda b: (b, 0, 0, 0, 0)),
                  pl.BlockSpec((None, None, KV_HEADS, n_cp, HEAD_DIM), lambda b: (b, 1, 0, 0, 0)),
                  pl.BlockSpec((n_cp, nbp), lambda b: (0, 0))],
        out_specs=[pl.BlockSpec((None, N_HEADS, HEAD_DIM), lambda b: (b, 0, 0)),
                   pl.BlockSpec((None, KV_HEADS, nbp), lambda b: (b, 0, 0))],
        out_shape=[jax.ShapeDtypeStruct((B, N_HEADS, HEAD_DIM), f32),
                   jax.ShapeDtypeStruct((B, KV_HEADS, nbp), f32)],
        compiler_params=_cparams(("parallel",)),
        name="nsa_select",
    )(q3, kc, kc, ov)


def _gather_copies(tok_ref, cache_ref, kvbuf, sems, b, n_gather):
    copies = []
    for k in range(KV_HEADS):
        for s in range(n_gather):
            off = pl.multiple_of(tok_ref[(b * KV_HEADS + k) * n_gather + s], SEL_LEN)
            src = cache_ref.at[pl.ds(off, SEL_LEN), pl.ds(2 * KV_HEADS, 2 * KV_HEADS), :]
            copies.append(pltpu.make_async_copy(src, kvbuf.at[k, pl.ds(s * SEL_LEN, SEL_LEN)], sems.at[k]))
    return copies


def _nsa_decode_kernel(tok_ref, cache_ref, q_ref, oc_ref, gn_ref, new_ref, win_ref, o_ref, kvbuf, sems,
                       *, n_gather):
    b = pl.program_id(0)
    n_keys = (n_gather + 1) * SEL_LEN
    new_col = n_gather * SEL_LEN
    for k in range(KV_HEADS):
        kvbuf[k, new_col:n_keys] = jnp.zeros((SEL_LEN, 2 * KV_HEADS, HEAD_DIM), f32)
    copies = _gather_copies(tok_ref, cache_ref, kvbuf, sems, b, n_gather)
    for c in copies:
        c.start()
    for c in copies:
        c.wait()

    q = q_ref[...]
    grp = _group_rows(N_HEADS)
    lane_s = lax.broadcasted_iota(i32, (1, n_keys), 1)
    n_win = win_ref.shape[0]
    lane_w = lax.broadcasted_iota(i32, (1, n_win), 1)
    o_s = jnp.zeros((N_HEADS, HEAD_DIM), f32)
    o_w = jnp.zeros((N_HEADS, HEAD_DIM), f32)

    def attend(keys, vals, k_new, v_new, lane, slot, mask):
        s = _dot_p(q, keys, _NT)
        s_new = jnp.sum(q * k_new, axis=-1, keepdims=True)
        s = jnp.where(lane == slot, s_new, s) * SCALE
        p = _masked_softmax_rows(s, mask)
        p_new = jnp.sum(jnp.where(lane == slot, p, 0.0), axis=-1, keepdims=True)
        p_old = jnp.where(lane == slot, 0.0, p)
        return _dot_p(p_old, vals) + p_new * v_new

    for k in range(KV_HEADS):
        o_k = attend(kvbuf[k, :, k, :], kvbuf[k, :, KV_HEADS + k, :], new_ref[2 * KV_HEADS + k:2 * KV_HEADS + k + 1, :],
                     new_ref[3 * KV_HEADS + k:3 * KV_HEADS + k + 1, :], lane_s, new_col, lane_s <= new_col)
        o_s = o_s + jnp.where(grp == k, o_k, 0.0)
        o_k = attend(win_ref[:, k, :], win_ref[:, KV_HEADS + k, :],
                     new_ref[4 * KV_HEADS + k:4 * KV_HEADS + k + 1, :],
                     new_ref[5 * KV_HEADS + k:5 * KV_HEADS + k + 1, :], lane_w, 0, lane_w >= 0)
        o_w = o_w + jnp.where(grp == k, o_k, 0.0)

    sig = jnp.broadcast_to(_sigmoid(gn_ref[...]), (N_HEADS, LANES))
    lane = lax.broadcasted_iota(i32, (N_HEADS, LANES), 1)
    head = lax.broadcasted_iota(i32, (N_HEADS, LANES), 0)
    out = None
    for br, o_br in enumerate((oc_ref[...], o_s, o_w)):
        gate = jnp.sum(jnp.where(lane == head * N_NSA_BRANCH + br, sig, 0.0), axis=-1, keepdims=True)
        term = gate * o_br
        out = term if out is None else out + term
    o_ref[...] = out.astype(o_ref.dtype)


def nsa_decode(tok_off, cache_tok, q3, o_c, gn3, new3, state_win5, layer, n_gather):
    B = q3.shape[0]
    n_win = state_win5.shape[2]
    n_keys = (n_gather + 1) * SEL_LEN
    kern = functools.partial(_nsa_decode_kernel, n_gather=n_gather)
    grid_spec = pltpu.PrefetchScalarGridSpec(
        num_scalar_prefetch=1,
        grid=(B,),
        in_specs=[pl.BlockSpec(memory_space=pl.ANY),
                  pl.BlockSpec((None, N_HEADS, HEAD_DIM), lambda b, t: (b, 0, 0)),
                  pl.BlockSpec((None, N_HEADS, HEAD_DIM), lambda b, t: (b, 0, 0)),
                  pl.BlockSpec((None, 1, LANES), lambda b, t: (b, 0, 0)),
                  pl.BlockSpec((None, 6 * KV_HEADS, HEAD_DIM), lambda b, t: (b, 0, 0)),
                  pl.BlockSpec((None, None, n_win, 2 * KV_HEADS, HEAD_DIM), lambda b, t: (layer, b, 0, 0, 0))],
        out_specs=pl.BlockSpec((None, N_HEADS, HEAD_DIM), lambda b, t: (b, 0, 0)),
        scratch_shapes=[pltpu.VMEM((KV_HEADS, n_keys, 2 * KV_HEADS, HEAD_DIM), f32),
                        pltpu.SemaphoreType.DMA((KV_HEADS,))],
    )
    return pl.pallas_call(
        kern,
        grid_spec=grid_spec,
        out_shape=jax.ShapeDtypeStruct((B, N_HEADS, HEAD_DIM), f32),
        compiler_params=_cparams(("arbitrary",)),
        name="nsa_decode",
    )(tok_off, cache_tok, q3, o_c, gn3, new3, state_win5)


def _rope_tables(pos):
    half = HEAD_DIM // 2
    inv = jnp.power(ROPE_THETA, -jnp.arange(0, HEAD_DIM, 2, dtype=f32) / HEAD_DIM)
    ang = pos.astype(f32)[:, None] * inv[None, :]
    cos, sin = jnp.cos(ang), jnp.sin(ang)
    return jnp.concatenate([cos, cos], axis=-1), jnp.concatenate([-sin, sin], axis=-1)


def _trunk(x, t0, past, p):
    B, T, D = x.shape
    M = B * T
    depth = p["w_in"].shape[0]
    d_pool = p["pool_scale"].shape[-1]
    d_main = d_pool + D_ATTN + 6 * D_KV
    gate_col = d_main
    mrg_col = d_main + N_HEADS * N_NSA_BRANCH
    tn_main = 512
    assert d_pool % tn_main == 0 and d_main % tn_main == 0 and gate_col % LANES == 0
    q_blk0 = d_pool // tn_main
    kv_blk0 = (d_pool + D_ATTN) // tn_main
    rope_blocks = tuple(range(q_blk0, kv_blk0)) + tuple(kv_blk0 + r for r in (0, 2, 4))
    kv_col = (d_pool + D_ATTN) // HEAD_DIM

    cos, sin = _rope_tables(t0 + jnp.arange(T))
    tm = _pick(M, 1024, 8)
    if T >= tm:
        n_tab = T // tm
    else:
        cos, sin = jnp.tile(cos, (tm // T, 1)), jnp.tile(sin, (tm // T, 1))
        n_tab = 1

    def rope_specs(tm_, tn_):
        assert tm_ == tm
        spec = pl.BlockSpec((tm, HEAD_DIM), lambda i, j, k: (i % n_tab, 0))
        return [spec, spec]

    x2 = x.reshape(M, D)
    w_mrg = p["w_in"][:, :, mrg_col:]
    wc = jnp.concatenate([p["cmp_w1"][:, :, :CMP_STRIDE], p["cmp_w1"][:, :, CMP_STRIDE:]], axis=-1)
    cb1 = p["cmp_b1"][:, :, None, :]
    if past is not None:
        cache_kv, page_table, state_win, state_pool = past
        n_pool = cache_kv.shape[1]
        n_pages = page_table.shape[1]
        past_len = n_pages * PAGE_SIZE
        assert T == 1 and t0 == past_len and state_win.shape[2] == WINDOW
        n_blk = past_len // SEL_LEN + 1
        assert n_blk >= N_SEL
        nbp = -(-n_blk // LANES) * LANES
        cache_pages = cache_kv.reshape(depth * n_pool, PAGE_SIZE, 4 * KV_HEADS, HEAD_DIM)
        cache_tok = cache_kv.reshape(depth * n_pool * PAGE_SIZE, 4 * KV_HEADS, HEAD_DIM)
        state_win5 = state_win.reshape(depth, B, WINDOW, 2 * KV_HEADS, HEAD_DIM)
        ov_s = _cmp_to_sel(past_len // CMP_STRIDE, n_blk, nbp)

    precise = past is not None
    act = f32 if precise else bf16
    kv_new, win_new, pool_new = [], [], []
    for l in range(depth):
        xn = rmsnorm(x2, p["norm_mix"], l, act)
        proj = _matmul(xn, [p["w_in"]], (l,), d_main, 0, [cos, sin], rope_specs,
                       functools.partial(_ep_inproj, rope_blocks=rope_blocks), f32,
                       tm_pref=tm, tn_pref=tn_main, name="in_proj", precise=precise)
        gn = _matmul(xn, [p["w_in"]], (l,), LANES, gate_col // LANES, [], _no_extras, _ep_plain, f32,
                     tm_pref=tm, tn_pref=LANES, name="gate_proj", precise=precise)
        g_mrg = _matmul(xn, [w_mrg], (l,), 2 * D, 0, [], _no_extras, _ep_plain, f32, tm_pref=tm,
                        name="merge_gate_proj", precise=precise)

        rows = proj[:, d_pool + D_ATTN:d_pool + D_ATTN + 4 * D_KV].reshape(B, T, 4, KV_HEADS, HEAD_DIM)
        wrows = proj[:, d_pool + D_ATTN + 4 * D_KV:d_main].reshape(B, T, 2, KV_HEADS, HEAD_DIM)
        u = proj[:, :d_pool].reshape(B, T, d_pool)
        if past is None:
            proj3 = proj.reshape(B, T, d_main)
            pre16 = jnp.zeros((B, POOL_KEEP + 1, d_pool), f32)
            pool_out = pool_prompt(proj3, pre16, p["pool_w"], p["pool_scale"], l, t0).reshape(M, d_pool)
            kc = compress_prompt(proj3, wc, cb1, p["cmp_w2"], l, kv_col)
            attn = nsa_prompt(proj3, gn.reshape(B, T, LANES), kc, q_blk0 // (GROUP * HEAD_DIM // tn_main), kv_col)
            attn = attn.reshape(M, D_ATTN)
            win_new.append(wrows[:, -min(WINDOW, T):])
            pool_new.append(u[:, -POOL_KEEP:])
        else:
            state_t = jnp.transpose(state_pool[l], (1, 0, 2))
            pool_out = pool_sample(proj, state_t, p["pool_w"], p["pool_scale"], l, t0)
            hbuf = compress_pages(cache_pages, page_table, wc, l, n_pool)
            kc = compress_finish(hbuf, cb1, p["cmp_w2"], l)
            q3 = proj[:, d_pool:d_pool + D_ATTN].reshape(B, N_HEADS, HEAD_DIM)
            o_c, selmask = nsa_select(q3, kc, ov_s, t0, n_blk)
            cur = t0 // SEL_LEN
            n_gather = N_SEL - 1
            others = selmask[:, :, :n_blk].at[:, :, cur].set(0.0)
            _, blk_idx = lax.top_k(others, n_gather)
            per_page = PAGE_SIZE // SEL_LEN
            page = jnp.take_along_axis(page_table, blk_idx.reshape(B, -1) // per_page, axis=1)
            tok_off = ((l * n_pool + page) * PAGE_SIZE + (blk_idx.reshape(B, -1) % per_page) * SEL_LEN)
            new3 = proj[:, d_pool + D_ATTN:d_main].reshape(B, 6 * KV_HEADS, HEAD_DIM)
            attn = nsa_decode(tok_off.reshape(-1).astype(i32), cache_tok, q3, o_c, gn.reshape(B, 1, LANES),
                              new3, state_win5, l, n_gather).reshape(M, D_ATTN)
            win_new.append(jnp.concatenate([state_win[l], wrows], axis=1)[:, -WINDOW:])
            pool_new.append(jnp.concatenate([state_pool[l], u], axis=1)[:, -POOL_KEEP:])
        kv_new.append(rows)

        pp = _matmul(pool_out, [p["w_pool_proj"]], (l,), D, 0, [], _no_extras, _ep_plain, f32, tm_pref=tm,
                     name="pool_proj", precise=precise)

        def merge_specs(tm_, tn_):
            nb = D // tn_
            return [pl.BlockSpec((tm_, tn_), lambda i, j, k: (i, j)),
                    pl.BlockSpec((tm_, tn_), lambda i, j, k: (i, j)),
                    pl.BlockSpec((tm_, tn_), lambda i, j, k: (i, j + nb))]

        h = _matmul(attn, [p["w_attn_proj"]], (l,), D, 0, [pp, g_mrg, g_mrg], merge_specs, _ep_merge, act,
                    tm_pref=tm, name="attn_proj_merge", precise=precise)

        def res_specs(tm_, tn_):
            return [pl.BlockSpec((tm_, tn_), lambda i, j, k: (i, j))]

        x2 = _matmul(h, [p["w_out"]], (l,), D, 0, [x2], res_specs, _ep_residual, f32, tm_pref=tm,
                     name="out_proj", precise=precise)

        xn = rmsnorm(x2, p["norm_ffn"], l, act if l % 2 == 0 else f32)
        if l % 2 == 0:
            hid = _matmul(xn, [p["ffn_wg"], p["ffn_wu"]], (l // 2,), p["ffn_wg"].shape[-1], 0, [], _no_extras,
                          _ep_swiglu, act, tm_pref=tm, name="ffn_up", precise=precise)
            x2 = _matmul(hid, [p["ffn_wd"]], (l // 2,), D, 0, [x2], res_specs, _ep_residual, f32,
                         tm_pref=tm, name="ffn_down", precise=precise)
        else:
            w_r = jnp.pad(p["moe_router"], ((0, 0), (0, 0), (0, LANES - N_EXPERTS)))
            comb = _matmul(xn, [w_r], (l // 2,), LANES, 0, [], _no_extras, _ep_router, f32, tm_pref=tm,
                           name="router", precise=precise)
            out = None
            if 2 * M >= N_EXPERTS * tm:
                gates, pos, src_tok, tile_expert, n_used = _route(comb, tm)
                xs = moe_dispatch(xn, src_tok)
                hid = _matmul(xs, [p["moe_wg"], p["moe_wu"]], (l // 2,), p["moe_wg"].shape[-1], 0, [], _no_extras,
                              _ep_swiglu, bf16, tm_pref=tm, name="moe_up_sorted", group=(tile_expert, n_used))
                ys = _matmul(hid, [p["moe_wd"]], (l // 2,), D, 0, [], _no_extras, _ep_plain, f32,
                             tm_pref=tm, name="moe_down_sorted", group=(tile_expert, n_used))
                out = moe_combine(ys, pos, jnp.pad(gates, ((0, 0), (0, LANES - 2))), x2)
            for e in range(N_EXPERTS if out is None else 0):
                hid = _matmul(xn, [p["moe_wg"], p["moe_wu"]], (l // 2, e), p["moe_wg"].shape[-1], 0, [], _no_extras,
                              _ep_swiglu, bf16, tm_pref=tm, name="moe_up")
                first, last = e == 0, e == N_EXPERTS - 1
                extras = [comb] + ([] if first else [out]) + ([x2] if last else [])

                def moe_specs(tm_, tn_, n=len(extras)):
                    return ([pl.BlockSpec((tm_, LANES), lambda i, j, k: (i, 0))]
                            + [pl.BlockSpec((tm_, tn_), lambda i, j, k: (i, j))] * (n - 1))

                out = _matmul(hid, [p["moe_wd"]], (l // 2, e), D, 0, extras, moe_specs,
                              functools.partial(_ep_moe_acc, expert=e, first=first, last=last), f32,
                              tm_pref=tm, name="moe_down")
            x2 = out

    y = rmsnorm(x2, p["norm_final"][None], 0, f32).reshape(B, T, D)
    return y, jnp.stack(kv_new, 0), jnp.stack(win_new, 0), jnp.stack(pool_new, 0)


def kernel(x_prompt, x_sample, cache_kv, state_win, state_pool, page_table, norm_mix, w_in, pool_w, pool_scale,
           cmp_w1, cmp_b1, cmp_w2, w_pool_proj, w_attn_proj, w_out, norm_ffn, ffn_wg, ffn_wu, ffn_wd,
           moe_router, moe_wg, moe_wu, moe_wd, norm_final):
    p = dict(norm_mix=norm_mix, w_in=w_in, pool_w=pool_w, pool_scale=pool_scale, cmp_w1=cmp_w1, cmp_b1=cmp_b1,
             cmp_w2=cmp_w2, w_pool_proj=w_pool_proj, w_attn_proj=w_attn_proj, w_out=w_out, norm_ffn=norm_ffn,
             ffn_wg=ffn_wg, ffn_wu=ffn_wu, ffn_wd=ffn_wd, moe_router=moe_router, moe_wg=moe_wg, moe_wu=moe_wu,
             moe_wd=moe_wd, norm_final=norm_final)
    y_p, kv_p, win_p, pool_p = _trunk(x_prompt, 0, None, p)
    past_len = page_table.shape[1] * PAGE_SIZE
    y_s, kv_s, win_s, pool_s = _trunk(x_sample, past_len, (cache_kv, page_table, state_win, state_pool), p)
    return (y_p, y_s, kv_p, kv_s, win_p, win_s, pool_p, pool_s)
```

```python
import functools
import math

import jax
import jax.numpy as jnp
from jax import lax
from jax.experimental import pallas as pl
from jax.experimental.pallas import tpu as pltpu

f32 = jnp.float32
bf16 = jnp.bfloat16
i32 = jnp.int32

N_HEADS = 16
HEAD_DIM = 128
KV_HEADS = 4
GROUP = N_HEADS // KV_HEADS
CMP_LEN = 32
CMP_STRIDE = 16
SEL_LEN = 64
N_SEL = 16
WINDOW = 512
N_NSA_BRANCH = 3
FORCE = 1e4
ROPE_THETA = 10000.0
POOL_WINDOWS = (2, 4, 8, 16)
POOL_KEEP = max(POOL_WINDOWS) - 1
N_EXPERTS = 8
EPS = 1e-6
NEG = -1e30
SCALE = HEAD_DIM ** -0.5
PAGE_SIZE = 128
LANES = 128
D_KV = KV_HEADS * HEAD_DIM
D_ATTN = N_HEADS * HEAD_DIM
VMEM_LIMIT = 56 * 1024 * 1024

_NT = (((1,), (1,)), ((), ()))


def _cparams(sem):
    return pltpu.CompilerParams(dimension_semantics=sem, vmem_limit_bytes=VMEM_LIMIT)


def _pick(dim, pref, mult=LANES):
    if dim <= pref:
        return dim
    t = (pref // mult) * mult
    while t >= mult:
        if dim % t == 0:
            return t
        t -= mult
    return dim


def _sigmoid(x):
    return 1.0 / (1.0 + jnp.exp(-x))


def _gelu_tanh(x):
    c = math.sqrt(2.0 / math.pi)
    return x * (0.5 * (1.0 + jnp.tanh(c * (x + 0.044715 * (x * x * x)))))


def _rmsnorm_kernel(x_ref, g_ref, o_ref):
    x = x_ref[...]
    ms = jnp.mean(x * x, axis=-1, keepdims=True)
    o_ref[...] = ((x * lax.rsqrt(ms + EPS)) * g_ref[...]).astype(o_ref.dtype)


def rmsnorm(x, gains, layer, out_dtype):
    M, D = x.shape
    tm = _pick(M, 512, 8)
    g3 = gains.reshape(gains.shape[0], 1, D)
    return pl.pallas_call(
        _rmsnorm_kernel,
        grid=(M // tm,),
        in_specs=[pl.BlockSpec((tm, D), lambda i: (i, 0)),
                  pl.BlockSpec((None, 1, D), lambda i: (layer, 0, 0))],
        out_specs=pl.BlockSpec((tm, D), lambda i: (i, 0)),
        out_shape=jax.ShapeDtypeStruct((M, D), out_dtype),
        compiler_params=_cparams(("parallel",)),
        name="rmsnorm",
    )(x, g3)


def _split(a):
    hi = a.astype(bf16)
    return hi, (a - hi.astype(f32)).astype(bf16)


def _dot_p(a, b, dims=None):
    dims = dims or (((a.ndim - 1,), (0,)), ((), ()))
    a_parts = _split(a) if a.dtype == f32 else (a,)
    b_parts = _split(b) if b.dtype == f32 else (b,)
    out = None
    for ia, ap in enumerate(a_parts):
        for ib, bp in enumerate(b_parts):
            if ia + ib < 2:
                t = lax.dot_general(ap, bp, dims, preferred_element_type=f32)
                out = t if out is None else out + t
    return out


def _mm_body(x_ref, w_refs, extras, o_ref, acc_refs, nk, epilogue, precise):
    j = pl.program_id(1)
    k = pl.program_id(2)
    if precise:
        parts = [_dot_p(x_ref[...], w_ref[...]) for w_ref in w_refs]
    else:
        x = x_ref[...].astype(bf16)
        parts = [jnp.dot(x, w_ref[...].astype(bf16), preferred_element_type=f32) for w_ref in w_refs]
    if nk == 1:
        epilogue(parts, extras, o_ref, j)
        return

    @pl.when(k == 0)
    def _():
        for a, p in zip(acc_refs, parts):
            a[...] = p

    @pl.when(k > 0)
    def _():
        for a, p in zip(acc_refs, parts):
            a[...] += p

    @pl.when(k == nk - 1)
    def _():
        epilogue([a[...] for a in acc_refs], extras, o_ref, j)


def _mm_kernel(*refs, n_w, n_extra, nk, epilogue, grouped, precise):
    if grouped:
        _, n_used_ref = refs[:2]
        refs = refs[2:]
    x_ref = refs[0]
    w_refs = refs[1:1 + n_w]
    extras = refs[1 + n_w:1 + n_w + n_extra]
    o_ref = refs[1 + n_w + n_extra]
    acc_refs = refs[2 + n_w + n_extra:]
    if not grouped:
        _mm_body(x_ref, w_refs, extras, o_ref, acc_refs, nk, epilogue, precise)
        return
    used = pl.program_id(0) < n_used_ref[0]

    @pl.when(used)
    def _():
        _mm_body(x_ref, w_refs, extras, o_ref, acc_refs, nk, epilogue, precise)

    @pl.when(jnp.logical_not(used))
    def _():
        o_ref[...] = jnp.zeros(o_ref.shape, o_ref.dtype)


def _matmul(x, ws, w_lead, n_cols, col_off_blocks, extras, extra_specs, epilogue, out_dtype,
            tm_pref=1024, tn_pref=512, tk_pref=2048, name="matmul", group=None, precise=False):
    M, K = x.shape
    tm = _pick(M, tm_pref, 8)
    tn = _pick(n_cols, tn_pref)
    tk = _pick(K, tk_pref)
    nk = K // tk
    grouped = group is not None
    n_lead = len(w_lead) + (1 if grouped else 0)

    def w_index(i, j, k, *pref):
        lead = tuple(w_lead) + ((pref[0][i],) if grouped else ())
        return lead + (k, j + col_off_blocks)

    w_spec = pl.BlockSpec((None,) * n_lead + (tk, tn), w_index)
    kern = functools.partial(_mm_kernel, n_w=len(ws), n_extra=len(extras), nk=nk, epilogue=epilogue,
                             grouped=grouped, precise=precise)
    scratch = [pltpu.VMEM((tm, tn), f32) for _ in ws] if nk > 1 else []
    grid_spec = pltpu.PrefetchScalarGridSpec(
        num_scalar_prefetch=2 if grouped else 0,
        grid=(M // tm, n_cols // tn, nk),
        in_specs=[pl.BlockSpec((tm, tk), lambda i, j, k, *_: (i, k))] + [w_spec] * len(ws) + list(extra_specs(tm, tn)),
        out_specs=pl.BlockSpec((tm, tn), lambda i, j, k, *_: (i, j)),
        scratch_shapes=scratch,
    )
    return pl.pallas_call(
        kern,
        grid_spec=grid_spec,
        out_shape=jax.ShapeDtypeStruct((M, n_cols), out_dtype),
        compiler_params=_cparams(("parallel", "parallel", "arbitrary")),
        name=name,
    )(*(group or ()), x, *ws, *extras)


def _no_extras(tm, tn):
    return []


def _ep_plain(accs, extras, o_ref, j):
    o_ref[...] = accs[0].astype(o_ref.dtype)


def _ep_residual(accs, extras, o_ref, j):
    o_ref[...] = (extras[0][...] + accs[0]).astype(o_ref.dtype)


def _ep_swiglu(accs, extras, o_ref, j):
    g, u = accs
    o_ref[...] = ((g * _sigmoid(g)) * u).astype(o_ref.dtype)


def _ep_merge(accs, extras, o_ref, j):
    p_ref, g0_ref, g1_ref = extras
    h = _sigmoid(g0_ref[...]) * p_ref[...] + _sigmoid(g1_ref[...]) * accs[0]
    o_ref[...] = h.astype(o_ref.dtype)


def _rope_tile(v, cos, sin):
    outs = []
    for h in range(v.shape[1] // HEAD_DIM):
        seg = v[:, h * HEAD_DIM:(h + 1) * HEAD_DIM]
        outs.append(seg * cos + pltpu.roll(seg, HEAD_DIM // 2, axis=1) * sin)
    return jnp.concatenate(outs, axis=1) if len(outs) > 1 else outs[0]


def _ep_inproj(accs, extras, o_ref, j, *, rope_blocks):
    cos_ref, sin_ref = extras
    flag = functools.reduce(jnp.logical_or, [j == b for b in rope_blocks])

    @pl.when(flag)
    def _():
        o_ref[...] = _rope_tile(accs[0], cos_ref[...], sin_ref[...]).astype(o_ref.dtype)

    @pl.when(jnp.logical_not(flag))
    def _():
        o_ref[...] = accs[0].astype(o_ref.dtype)


def _ep_router(accs, extras, o_ref, j):
    logits = accs[0]
    lane = lax.broadcasted_iota(i32, logits.shape, 1)
    lg = jnp.where(lane < N_EXPERTS, logits, -jnp.inf)
    m1 = jnp.max(lg, axis=-1, keepdims=True)
    i1 = jnp.min(jnp.where(lg == m1, lane, LANES), axis=-1, keepdims=True)
    lg2 = jnp.where(lane == i1, -jnp.inf, lg)
    m2 = jnp.max(lg2, axis=-1, keepdims=True)
    i2 = jnp.min(jnp.where(lg2 == m2, lane, LANES), axis=-1, keepdims=True)
    e2 = jnp.exp(m2 - m1)
    den = 1.0 + e2
    comb = jnp.where(lane == i1, 1.0 / den, 0.0) + jnp.where(lane == i2, e2 / den, 0.0)
    o_ref[...] = comb.astype(o_ref.dtype)


def _ep_moe_acc(accs, extras, o_ref, j, *, expert, first, last):
    lane_src = extras[0][...]
    lane = lax.broadcasted_iota(i32, lane_src.shape, 1)
    c = jnp.sum(jnp.where(lane == expert, lane_src, 0.0), axis=-1, keepdims=True)
    out = c * accs[0]
    nxt = 1
    if not first:
        out = extras[nxt][...] + out
        nxt += 1
    if last:
        out = extras[nxt][...] + out
    o_ref[...] = out.astype(o_ref.dtype)


def _route(comb, tm):
    M = comb.shape[0]
    gates, idx = lax.top_k(comb[:, :N_EXPERTS], 2)
    e_flat = idx.reshape(-1).astype(i32)
    onehot = (e_flat[:, None] == jnp.arange(N_EXPERTS, dtype=i32)[None, :]).astype(i32)
    csum = jnp.cumsum(onehot, axis=0)
    rank = jnp.sum((csum - onehot) * onehot, axis=1)
    tiles_e = (csum[-1] + tm - 1) // tm
    tile_end = jnp.cumsum(tiles_e)
    pos = (tile_end - tiles_e)[e_flat] * tm + rank
    n_tiles = 2 * M // tm + N_EXPERTS
    src_tok = (jnp.arange(n_tiles * tm, dtype=i32) % M).at[pos].set(jnp.arange(2 * M, dtype=i32) // 2)
    n_used = tile_end[-1]
    t = jnp.minimum(jnp.arange(n_tiles, dtype=i32), n_used - 1)
    tile_expert = jnp.minimum(jnp.searchsorted(tile_end, t, side="right"), N_EXPERTS - 1).astype(i32)
    return gates, pos.reshape(M, 2).astype(i32), src_tok, tile_expert, n_used.reshape(1).astype(i32)


def _row_copy(src_hbm, row, dst, r, sem):
    return pltpu.make_async_copy(src_hbm.at[pl.ds(row, 1), :], dst.at[pl.ds(r, 1), :], sem)


def _dispatch_kernel(src_ref, used_ref, x_ref, o_ref, buf, sem, *, rows, tm):
    base = pl.program_id(0) * rows
    used = base < used_ref[0] * tm

    def start(r, c):
        _row_copy(x_ref, src_ref[base + r], buf, r, sem.at[0]).start()
        return c

    def wait(r, c):
        _row_copy(x_ref, src_ref[base + r], buf, r, sem.at[0]).wait()
        return c

    @pl.when(used)
    def _():
        lax.fori_loop(0, rows, start, 0, unroll=8)
        lax.fori_loop(0, rows, wait, 0, unroll=8)
        o_ref[...] = buf[...].astype(o_ref.dtype)

    @pl.when(jnp.logical_not(used))
    def _():
        o_ref[...] = jnp.zeros(o_ref.shape, o_ref.dtype)


def moe_dispatch(x, src_tok, n_used, tm):
    P = src_tok.shape[0]
    D = x.shape[1]
    rows = _pick(tm, 256, 8)
    grid_spec = pltpu.PrefetchScalarGridSpec(
        num_scalar_prefetch=2,
        grid=(P // rows,),
        in_specs=[pl.BlockSpec(memory_space=pl.ANY)],
        out_specs=pl.BlockSpec((rows, D), lambda i, s, u: (i, 0)),
        scratch_shapes=[pltpu.VMEM((rows, D), f32), pltpu.SemaphoreType.DMA((1,))],
    )
    return pl.pallas_call(
        functools.partial(_dispatch_kernel, rows=rows, tm=tm),
        grid_spec=grid_spec,
        out_shape=jax.ShapeDtypeStruct((P, D), bf16),
        compiler_params=_cparams(("arbitrary",)),
        name="moe_dispatch",
    )(src_tok, n_used, x)


def _combine_kernel(pos_ref, y_ref, x_ref, g_ref, o_ref, buf, sem, *, rows):
    base = pl.program_id(0) * rows

    def start(r, c):
        for j in range(2):
            _row_copy(y_ref, pos_ref[2 * (base + r) + j], buf.at[j], r, sem.at[j]).start()
        return c

    def wait(r, c):
        for j in range(2):
            _row_copy(y_ref, pos_ref[2 * (base + r) + j], buf.at[j], r, sem.at[j]).wait()
        return c

    lax.fori_loop(0, rows, start, 0, unroll=4)
    lax.fori_loop(0, rows, wait, 0, unroll=4)
    g = g_ref[...]
    lane = lax.broadcasted_iota(i32, g.shape, 1)
    g0 = jnp.sum(jnp.where(lane == 0, g, 0.0), axis=-1, keepdims=True)
    g1 = jnp.sum(jnp.where(lane == 1, g, 0.0), axis=-1, keepdims=True)
    o_ref[...] = x_ref[...] + (g0 * buf[0] + g1 * buf[1])


def moe_combine(y_sorted, pos, gates_pad, x):
    M, D = x.shape
    rows = _pick(M, 256, 8)
    grid_spec = pltpu.PrefetchScalarGridSpec(
        num_scalar_prefetch=1,
        grid=(M // rows,),
        in_specs=[pl.BlockSpec(memory_space=pl.ANY),
                  pl.BlockSpec((rows, D), lambda i, s: (i, 0)),
                  pl.BlockSpec((rows, LANES), lambda i, s: (i, 0))],
        out_specs=pl.BlockSpec((rows, D), lambda i, s: (i, 0)),
        scratch_shapes=[pltpu.VMEM((2, rows, D), f32), pltpu.SemaphoreType.DMA((2,))],
    )
    return pl.pallas_call(
        functools.partial(_combine_kernel, rows=rows),
        grid_spec=grid_spec,
        out_shape=jax.ShapeDtypeStruct((M, D), f32),
        compiler_params=_cparams(("arbitrary",)),
        name="moe_combine",
    )(pos.reshape(-1), y_sorted, x, gates_pad)


def _pool_groups(shifted, u, pos1, pw_ref, sc_ref, o_ref, precise=False):
    pg = u.shape[1] // len(POOL_WINDOWS)
    for g, w in enumerate(POOL_WINDOWS):
        c = slice(g * pg, (g + 1) * pg)
        s = u[:, c]
        for jj in range(1, w):
            s = s + shifted(jj, c)
        cnt = jnp.minimum(float(w), pos1)
        pooled = s / cnt - u[:, c]
        if precise:
            mixed = _dot_p(pooled, pw_ref[g])
        else:
            mixed = jnp.dot(pooled.astype(bf16), pw_ref[g].astype(bf16), preferred_element_type=f32)
        o_ref[:, c] = (mixed * sc_ref[:, c]).astype(o_ref.dtype)


def _pool_prompt_kernel(u_ref, pre_ref, pw_ref, sc_ref, o_ref, ext_ref, *, tp, t0):
    i = pl.program_id(1)
    halo = POOL_KEEP + 1

    @pl.when(i == 0)
    def _():
        ext_ref[0:halo, :] = pre_ref[...]

    u = u_ref[...]
    ext_ref[halo:halo + tp, :] = u
    pos1 = (lax.broadcasted_iota(i32, (tp, 1), 0) + (i * tp + t0 + 1)).astype(f32)
    _pool_groups(lambda jj, c: ext_ref[halo - jj:halo - jj + tp, c], u, pos1, pw_ref, sc_ref, o_ref)
    ext_ref[0:halo, :] = ext_ref[tp:tp + halo, :]


def pool_prompt(proj3, prefix16, pool_w, pool_scale, layer, t0):
    B, T, _ = proj3.shape
    dp = pool_scale.shape[-1]
    tp = _pick(T, 512, 16)
    kern = functools.partial(_pool_prompt_kernel, tp=tp, t0=t0)
    sc3 = pool_scale.reshape(pool_scale.shape[0], 1, dp)
    return pl.pallas_call(
        kern,
        grid=(B, T // tp),
        in_specs=[pl.BlockSpec((None, tp, dp), lambda b, i: (b, i, 0)),
                  pl.BlockSpec((None, POOL_KEEP + 1, dp), lambda b, i: (b, 0, 0)),
                  pl.BlockSpec((None,) + pool_w.shape[1:], lambda b, i: (layer, 0, 0, 0)),
                  pl.BlockSpec((None, 1, dp), lambda b, i: (layer, 0, 0))],
        out_specs=pl.BlockSpec((None, tp, dp), lambda b, i: (b, i, 0)),
        out_shape=jax.ShapeDtypeStruct((B, T, dp), bf16),
        scratch_shapes=[pltpu.VMEM((tp + POOL_KEEP + 1, dp), f32)],
        compiler_params=_cparams(("arbitrary", "arbitrary")),
        name="pool_prompt",
    )(proj3, prefix16, pool_w, sc3)


def _pool_sample_kernel(u_ref, st_ref, pw_ref, sc_ref, o_ref, *, t0):
    u = u_ref[...]
    pos1 = jnp.full((u.shape[0], 1), float(t0 + 1), f32)
    _pool_groups(lambda jj, c: st_ref[POOL_KEEP - jj, :, c], u, pos1, pw_ref, sc_ref, o_ref, precise=True)


def pool_sample(proj, state_t, pool_w, pool_scale, layer, t0):
    B = proj.shape[0]
    dp = pool_scale.shape[-1]
    kern = functools.partial(_pool_sample_kernel, t0=t0)
    sc3 = pool_scale.reshape(pool_scale.shape[0], 1, dp)
    return pl.pallas_call(
        kern,
        grid=(1,),
        in_specs=[pl.BlockSpec((B, dp), lambda i: (0, 0)),
                  pl.BlockSpec((POOL_KEEP, B, dp), lambda i: (0, 0, 0)),
                  pl.BlockSpec((None,) + pool_w.shape[1:], lambda i: (layer, 0, 0, 0)),
                  pl.BlockSpec((None, 1, dp), lambda i: (layer, 0, 0))],
        out_specs=pl.BlockSpec((B, dp), lambda i: (0, 0)),
        out_shape=jax.ShapeDtypeStruct((B, dp), f32),
        compiler_params=_cparams(("arbitrary",)),
        name="pool_sample",
    )(proj, state_t, pool_w, sc3)


def _compress_tail(h01, b1, w2, step=1, precise=False):
    n = h01.shape[0]
    hd = h01.shape[1] // 2
    hid = h01[:, :hd] + pltpu.roll(h01[:, hd:], n - step, axis=0) + b1
    if precise:
        return _dot_p(_gelu_tanh(hid), w2)
    return jnp.dot(_gelu_tanh(hid).astype(bf16), w2.astype(bf16), preferred_element_type=f32)


def _compress_prompt_kernel(x_ref, wc_ref, b1_ref, w2_ref, o_ref, *, n_sub):
    acc = None
    for l in range(CMP_STRIDE):
        xl = x_ref[pl.ds(l, n_sub, stride=CMP_STRIDE), :].astype(bf16)
        p = jnp.dot(xl, wc_ref[l].astype(bf16), preferred_element_type=f32)
        acc = p if acc is None else acc + p
    o_ref[...] = _compress_tail(acc, b1_ref[...], w2_ref[...]).astype(o_ref.dtype)


def compress_prompt(proj3, wc, b1, w2, layer, kv_col_block):
    B, T, _ = proj3.shape
    n_sub = T // CMP_STRIDE
    hid2 = wc.shape[-1]
    kern = functools.partial(_compress_prompt_kernel, n_sub=n_sub)
    return pl.pallas_call(
        kern,
        grid=(B, 2, KV_HEADS),
        in_specs=[pl.BlockSpec((None, T, HEAD_DIM), lambda b, r, h: (b, 0, kv_col_block + r * KV_HEADS + h)),
                  pl.BlockSpec((None, None, CMP_STRIDE, HEAD_DIM, hid2), lambda b, r, h: (layer, r, 0, 0, 0)),
                  pl.BlockSpec((None, None, 1, hid2 // 2), lambda b, r, h: (layer, r, 0, 0)),
                  pl.BlockSpec((None, None, hid2 // 2, HEAD_DIM), lambda b, r, h: (layer, r, 0, 0))],
        out_specs=pl.BlockSpec((None, None, None, n_sub, HEAD_DIM), lambda b, r, h: (b, r, h, 0, 0)),
        out_shape=jax.ShapeDtypeStruct((B, 2, KV_HEADS, n_sub, HEAD_DIM), bf16),
        compiler_params=_cparams(("parallel", "parallel", "parallel")),
        name="compress_prompt",
    )(proj3, wc, b1, w2)


def _compress_pages_kernel(pt_ref, *refs, pg):
    page_refs = refs[:pg]
    wc_ref = refs[pg]
    o_ref = refs[pg + 1]
    sub = PAGE_SIZE // CMP_STRIDE
    is_k = lax.broadcasted_iota(i32, (2 * KV_HEADS, HEAD_DIM), 0) < KV_HEADS
    acc = [None, None]
    for l2 in range(CMP_STRIDE // 2):
        cols = ([], [])
        for l in (2 * l2, 2 * l2 + 1):
            k_tiles, v_tiles = [], []
            for s in range(pg):
                tok = page_refs[s][pl.ds(l, sub, stride=CMP_STRIDE)]
                for m in range(0, sub, 2):
                    a, b = tok[m], tok[m + 1]
                    k_tiles.append(jnp.where(is_k, a, pltpu.roll(b, KV_HEADS, axis=0)))
                    v_tiles.append(jnp.where(is_k, pltpu.roll(a, KV_HEADS, axis=0), b))
            cols[0].append(jnp.concatenate(k_tiles, axis=0))
            cols[1].append(jnp.concatenate(v_tiles, axis=0))
        for r in range(2):
            x2 = jnp.concatenate(cols[r], axis=1)
            p = _dot_p(x2, wc_ref[r, l2])
            acc[r] = p if acc[r] is None else acc[r] + p
    for r in range(2):
        o_ref[r] = acc[r]


def compress_pages(cache_pages, page_table, wc, layer, n_pool):
    B, n_pages = page_table.shape
    pg = math.gcd(n_pages, 8)
    sub = PAGE_SIZE // CMP_STRIDE
    hid2 = wc.shape[-1]
    wc2 = wc.reshape(wc.shape[0], 2, CMP_STRIDE // 2, 2 * HEAD_DIM, hid2)
    kern = functools.partial(_compress_pages_kernel, pg=pg)

    def page_spec(s):
        return pl.BlockSpec((None, PAGE_SIZE, 2 * KV_HEADS, HEAD_DIM),
                            lambda b, p, pt: (layer * n_pool + pt[b, p * pg + s], 0, 0, 0))

    grid_spec = pltpu.PrefetchScalarGridSpec(
        num_scalar_prefetch=1,
        grid=(B, n_pages // pg),
        in_specs=[page_spec(s) for s in range(pg)]
        + [pl.BlockSpec((None, 2, CMP_STRIDE // 2, 2 * HEAD_DIM, hid2), lambda b, p, pt: (layer, 0, 0, 0, 0))],
        out_specs=pl.BlockSpec((None, 2, pg * sub * KV_HEADS, hid2), lambda b, p, pt: (b, 0, p, 0)),
    )
    return pl.pallas_call(
        kern,
        grid_spec=grid_spec,
        out_shape=jax.ShapeDtypeStruct((B, 2, n_pages * sub * KV_HEADS, hid2), f32),
        compiler_params=_cparams(("parallel", "parallel")),
        name="compress_pages",
    )(page_table, *([cache_pages] * pg), wc2)


def _compress_finish_kernel(h_ref, b1_ref, w2_ref, o_ref):
    o_ref[...] = _compress_tail(h_ref[...], b1_ref[...], w2_ref[...], KV_HEADS, precise=True).astype(o_ref.dtype)


def compress_finish(hbuf, b1, w2, layer):
    B, _, n_rows, hid2 = hbuf.shape
    return pl.pallas_call(
        _compress_finish_kernel,
        grid=(B, 2),
        in_specs=[pl.BlockSpec((None, None, n_rows, hid2), lambda b, r: (b, r, 0, 0)),
                  pl.BlockSpec((None, None, 1, hid2 // 2), lambda b, r: (layer, r, 0, 0)),
                  pl.BlockSpec((None, None, hid2 // 2, HEAD_DIM), lambda b, r: (layer, r, 0, 0))],
        out_specs=pl.BlockSpec((None, None, n_rows, HEAD_DIM), lambda b, r: (b, r, 0, 0)),
        out_shape=jax.ShapeDtypeStruct((B, 2, n_rows, HEAD_DIM), f32),
        compiler_params=_cparams(("parallel", "parallel")),
        name="compress_finish",
    )(hbuf, b1, w2)


def _masked_softmax(s, mask, axis):
    s = jnp.where(mask, s, NEG)
    m = jnp.max(s, axis=axis, keepdims=True)
    e = jnp.where(mask, jnp.exp(s - m), 0.0)
    l = jnp.sum(e, axis=axis, keepdims=True)
    return e * jnp.where(l > 0.0, 1.0 / l, 0.0)


def _masked_softmax_rows(s, mask):
    return _masked_softmax(s, mask, -1)


def _nsa_prompt_kernel(q_ref, gn_ref, kc_ref, vc_ref, ks_ref, vs_ref, kw_ref, vw_ref, ovt_ref, ext_ref,
                       o_ref, ksb, vst, kwb, vwt, vct, m_sc, l_sc, acc_sc, *, tq, n_blk, kts, wlen, seq):
    kh = pl.program_id(1)
    i = pl.program_id(2)
    t0 = i * tq
    cols = GROUP * tq

    @pl.when(i == 0)
    def _():
        ch = min(seq, 512)
        for c in range(seq // ch):
            sl = slice(c * ch, (c + 1) * ch)
            ksb[sl, :] = ks_ref[sl, :].astype(bf16)
            kwb[sl, :] = kw_ref[sl, :].astype(bf16)
            vst[:, sl] = vs_ref[sl, :].T.astype(bf16)
            vwt[:, sl] = vw_ref[sl, :].T.astype(bf16)
        vct[...] = vc_ref[...].astype(f32).T.astype(bf16)

    q_all = jnp.concatenate([q_ref[:, g * HEAD_DIM:(g + 1) * HEAD_DIM] for g in range(GROUP)], axis=0).astype(bf16)
    t_q = t0 + lax.broadcasted_iota(i32, (1, tq), 1)
    t_col = jnp.concatenate([t_q] * GROUP, axis=1)

    n_cp = kc_ref.shape[0]
    s = lax.dot_general(kc_ref[...], q_all, _NT, preferred_element_type=f32) * SCALE
    cmp_end = lax.broadcasted_iota(i32, (n_cp, 1), 0) * CMP_STRIDE + (CMP_LEN - 1)
    p_c = _masked_softmax(s, cmp_end <= t_col, 0).astype(bf16)
    o_c = jnp.dot(vct[...], p_c, preferred_element_type=f32)

    imp_g = jnp.dot(ovt_ref[...], p_c, preferred_element_type=f32)
    imp = imp_g[:, 0:tq]
    for g in range(1, GROUP):
        imp = imp + imp_g[:, g * tq:(g + 1) * tq]
    nbr = imp.shape[0]
    blk = lax.broadcasted_iota(i32, (nbr, 1), 0)
    cur = t_q // SEL_LEN
    valid = blk * SEL_LEN <= t_q
    forced = (blk == 0) | (blk == cur) | (blk == cur - 1)
    imp = jnp.where(valid, jnp.where(forced, FORCE, imp), -jnp.inf)
    rank = jnp.zeros(imp.shape, f32)
    for ii in range(n_blk):
        row = imp[ii:ii + 1, :]
        beats = (row > imp) | ((row == imp) & (blk > ii))
        rank = rank + jnp.where(beats, 1.0, 0.0)
    sel = jnp.where((rank < float(N_SEL)) & valid, 1.0, 0.0).astype(bf16)

    m_sc[...] = jnp.full(m_sc.shape, NEG, f32)
    l_sc[...] = jnp.zeros(l_sc.shape, f32)
    acc_sc[...] = jnp.zeros(acc_sc.shape, f32)

    def body(kt, carry):
        off = pl.multiple_of(kt * kts, kts)
        sc = lax.dot_general(ksb[pl.ds(off, kts), :], q_all, _NT, preferred_element_type=f32) * SCALE
        hit = jnp.dot(ext_ref[kt], sel, preferred_element_type=f32)
        kpos = off + lax.broadcasted_iota(i32, (kts, 1), 0)
        bias = jnp.where((hit > 0.5) & (kpos <= t_q), 0.0, NEG)
        sc = sc + jnp.concatenate([bias] * GROUP, axis=1)
        m_old = m_sc[...]
        m_new = jnp.maximum(m_old, jnp.max(sc, axis=0, keepdims=True))
        alpha = jnp.exp(m_old - m_new)
        e = jnp.exp(sc - m_new)
        l_sc[...] = alpha * l_sc[...] + jnp.sum(e, axis=0, keepdims=True)
        acc_sc[...] = alpha * acc_sc[...] + jnp.dot(vst[:, pl.ds(off, kts)], e.astype(bf16),
                                                    preferred_element_type=f32)
        m_sc[...] = m_new
        return carry

    lax.fori_loop(0, (t0 + tq - 1) // kts + 1, body, 0)
    l = l_sc[...]
    o_s = acc_sc[...] * jnp.where(l > 0.0, 1.0 / l, 0.0)

    woff = pl.multiple_of(jnp.clip(t0 + tq - wlen, 0, seq - wlen), tq)
    sc = lax.dot_general(kwb[pl.ds(woff, wlen), :], q_all, _NT, preferred_element_type=f32) * SCALE
    kpos = woff + lax.broadcasted_iota(i32, (wlen, 1), 0)
    bias = jnp.where((kpos <= t_q) & (kpos > t_q - WINDOW), 0.0, NEG)
    sc = sc + jnp.concatenate([bias] * GROUP, axis=1)
    e = jnp.exp(sc - jnp.max(sc, axis=0, keepdims=True))
    p_w = (e * (1.0 / jnp.sum(e, axis=0, keepdims=True))).astype(bf16)
    o_w = jnp.dot(vwt[:, pl.ds(woff, wlen)], p_w, preferred_element_type=f32)

    sig_t = _sigmoid(gn_ref[...]).T
    gcol = lax.broadcasted_iota(i32, sig_t.shape, 0)
    for g in range(GROUP):
        c = slice(g * tq, (g + 1) * tq)
        out = None
        for br, o_br in enumerate((o_c, o_s, o_w)):
            colid = kh * (GROUP * N_NSA_BRANCH) + g * N_NSA_BRANCH + br
            gate = jnp.sum(jnp.where(gcol == colid, sig_t, 0.0), axis=0, keepdims=True)
            term = gate * o_br[:, c]
            out = term if out is None else out + term
        o_ref[:, g * HEAD_DIM:(g + 1) * HEAD_DIM] = out.T.astype(o_ref.dtype)


def _cmp_to_sel(n_rows, n_blk, n_cols):
    i0 = jnp.arange(n_rows)[:, None] * CMP_STRIDE
    j0 = jnp.arange(n_cols)[None, :] * SEL_LEN
    ov = jnp.clip(jnp.minimum(i0 + CMP_LEN, j0 + SEL_LEN) - jnp.maximum(i0, j0), 0, None)
    ov = jnp.where(jnp.arange(n_cols)[None, :] < n_blk, ov, 0)
    return (ov.astype(f32) / CMP_LEN).astype(bf16)


def nsa_prompt(proj3, gn3, kc, q_col, kv_col):
    B, T, _ = proj3.shape
    assert T % SEL_LEN == 0
    tq = _pick(T, 128, 64)
    kts = _pick(T, 512, tq)
    wlen = min(WINDOW + tq, T)
    assert T % tq == 0 and WINDOW % tq == 0
    n_blk = T // SEL_LEN
    nbr = -(-n_blk // SEL_LEN) * SEL_LEN
    n_cp = T // CMP_STRIDE
    ovt = _cmp_to_sel(n_cp, n_blk, nbr).T
    tok_blk = jnp.arange(T) // SEL_LEN
    ext = (tok_blk[:, None] == jnp.arange(nbr)[None, :]).astype(bf16).reshape(T // kts, kts, nbr)
    cols = GROUP * tq
    kern = functools.partial(_nsa_prompt_kernel, tq=tq, n_blk=n_blk, kts=kts, wlen=wlen, seq=T)

    def kv_spec(r):
        return pl.BlockSpec((None, T, HEAD_DIM), lambda b, k, i: (b, 0, kv_col + r * KV_HEADS + k))

    def cmp_spec(r):
        return pl.BlockSpec((None, None, None, n_cp, HEAD_DIM), lambda b, k, i: (b, r, k, 0, 0))

    return pl.pallas_call(
        kern,
        grid=(B, KV_HEADS, T // tq),
        in_specs=[pl.BlockSpec((None, tq, GROUP * HEAD_DIM), lambda b, k, i: (b, i, q_col + k)),
                  pl.BlockSpec((None, tq, LANES), lambda b, k, i: (b, i, 0)),
                  cmp_spec(0), cmp_spec(1), kv_spec(2), kv_spec(3), kv_spec(4), kv_spec(5),
                  pl.BlockSpec((nbr, n_cp), lambda b, k, i: (0, 0)),
                  pl.BlockSpec((T // kts, kts, nbr), lambda b, k, i: (0, 0, 0))],
        out_specs=pl.BlockSpec((None, tq, GROUP * HEAD_DIM), lambda b, k, i: (b, i, k)),
        out_shape=jax.ShapeDtypeStruct((B, T, D_ATTN), bf16),
        scratch_shapes=[pltpu.VMEM((T, HEAD_DIM), bf16), pltpu.VMEM((HEAD_DIM, T), bf16),
                        pltpu.VMEM((T, HEAD_DIM), bf16), pltpu.VMEM((HEAD_DIM, T), bf16),
                        pltpu.VMEM((HEAD_DIM, n_cp), bf16),
                        pltpu.VMEM((1, cols), f32), pltpu.VMEM((1, cols), f32), pltpu.VMEM((HEAD_DIM, cols), f32)],
        compiler_params=_cparams(("parallel", "parallel", "arbitrary")),
        name="nsa_prompt",
    )(proj3, gn3, kc, kc, proj3, proj3, proj3, proj3, ovt, ext)


def _group_rows(n_rows):
    return lax.broadcasted_iota(i32, (n_rows, 1), 0) // GROUP


def _nsa_select_kernel(q_ref, kc_ref, vc_ref, ov_ref, oc_ref, sel_ref, *, t_pos, n_blk):
    q = q_ref[...]
    n_rows = kc_ref.shape[0]
    nbp = ov_ref.shape[1]
    grp = _group_rows(N_HEADS)
    crow = lax.broadcasted_iota(i32, (1, n_rows), 1)
    cmp_end = (crow // KV_HEADS) * CMP_STRIDE + (CMP_LEN - 1)
    s = _dot_p(q, kc_ref[...], _NT) * SCALE
    p = _masked_softmax_rows(s, (cmp_end <= t_pos) & (crow % KV_HEADS == grp))
    oc_ref[...] = _dot_p(p, vc_ref[...])
    imp_h = _dot_p(p, ov_ref[...])
    blk = lax.broadcasted_iota(i32, (1, nbp), 1)
    cur = t_pos // SEL_LEN
    valid = (blk * SEL_LEN <= t_pos) & (blk < n_blk)
    forced = (blk == 0) | (blk == cur) | (blk == cur - 1)
    ri = lax.broadcasted_iota(i32, (nbp, nbp), 0)
    ci = lax.broadcasted_iota(i32, (nbp, nbp), 1)
    slot = lax.broadcasted_iota(i32, (1, LANES), 1).astype(f32)
    row_id = lax.broadcasted_iota(i32, (nbp, 1), 0).astype(f32)
    for k in range(KV_HEADS):
        imp = jnp.sum(jnp.where(grp == k, imp_h, 0.0), axis=0, keepdims=True)
        imp = jnp.where(valid, jnp.where(forced, FORCE, imp), -jnp.inf)
        a = jnp.broadcast_to(imp, (nbp, nbp))
        at = jnp.broadcast_to(jnp.sum(jnp.where(ri == ci, a, 0.0), axis=1, keepdims=True), (nbp, nbp))
        beats = (at > a) | ((at == a) & (ri < ci))
        rank = jnp.sum(jnp.where(beats, 1.0, 0.0), axis=0, keepdims=True)
        sel = jnp.where((rank < float(N_SEL)) & valid & (blk != cur), 1.0, 0.0)
        sel_b = jnp.broadcast_to(sel, (nbp, nbp))
        sel_col = jnp.sum(jnp.where(ri == ci, sel_b, 0.0), axis=1, keepdims=True)
        before = jnp.sum(jnp.where(ci < ri, sel_b, 0.0), axis=1, keepdims=True)
        hit = (before == slot) & (sel_col > 0.5)
        sel_ref[k:k + 1, :] = jnp.sum(jnp.where(hit, row_id, 0.0), axis=0, keepdims=True).astype(i32)


def nsa_select(q3, kc, ov, t_pos, n_blk):
    B = q3.shape[0]
    n_rows = kc.shape[2]
    nbp = ov.shape[1]
    kern = functools.partial(_nsa_select_kernel, t_pos=t_pos, n_blk=n_blk)
    return pl.pallas_call(
        kern,
        grid=(B,),
        in_specs=[pl.BlockSpec((None, N_HEADS, HEAD_DIM), lambda b: (b, 0, 0)),
                  pl.BlockSpec((None, None, n_rows, HEAD_DIM), lambda b: (b, 0, 0, 0)),
                  pl.BlockSpec((None, None, n_rows, HEAD_DIM), lambda b: (b, 1, 0, 0)),
                  pl.BlockSpec((n_rows, nbp), lambda b: (0, 0))],
        out_specs=[pl.BlockSpec((None, N_HEADS, HEAD_DIM), lambda b: (b, 0, 0)),
                   pl.BlockSpec((None, KV_HEADS, LANES), lambda b: (b, 0, 0))],
        out_shape=[jax.ShapeDtypeStruct((B, N_HEADS, HEAD_DIM), f32),
                   jax.ShapeDtypeStruct((B, KV_HEADS, LANES), i32)],
        compiler_params=_cparams(("parallel",)),
        name="nsa_select",
    )(q3, kc, kc, ov)


def _gather_copies(tok_ref, cache_ref, kvbuf, sems, b, n_gather):
    copies = []
    for k in range(KV_HEADS):
        for s in range(n_gather):
            off = pl.multiple_of(tok_ref[(b * KV_HEADS + k) * n_gather + s], SEL_LEN)
            src = cache_ref.at[pl.ds(off, SEL_LEN), pl.ds(2 * KV_HEADS, 2 * KV_HEADS), :]
            copies.append(pltpu.make_async_copy(src, kvbuf.at[k, pl.ds(s * SEL_LEN, SEL_LEN)], sems.at[k]))
    return copies


def _nsa_decode_kernel(tok_ref, cache_ref, q_ref, oc_ref, gn_ref, new_ref, win_ref, o_ref, kvbuf, sems,
                       *, n_gather):
    b = pl.program_id(0)
    n_keys = (n_gather + 1) * SEL_LEN
    new_col = n_gather * SEL_LEN
    for k in range(KV_HEADS):
        kvbuf[k, new_col:n_keys] = jnp.zeros((SEL_LEN, 2 * KV_HEADS, HEAD_DIM), f32)
    copies = _gather_copies(tok_ref, cache_ref, kvbuf, sems, b, n_gather)
    for c in copies:
        c.start()
    for c in copies:
        c.wait()

    q = q_ref[...]
    grp = _group_rows(N_HEADS)
    lane_s = lax.broadcasted_iota(i32, (1, n_keys), 1)
    n_win = win_ref.shape[0]
    lane_w = lax.broadcasted_iota(i32, (1, n_win), 1)
    o_s = jnp.zeros((N_HEADS, HEAD_DIM), f32)
    o_w = jnp.zeros((N_HEADS, HEAD_DIM), f32)

    def attend(keys, vals, k_new, v_new, lane, slot, mask):
        s = _dot_p(q, keys, _NT)
        s_new = jnp.sum(q * k_new, axis=-1, keepdims=True)
        s = jnp.where(lane == slot, s_new, s) * SCALE
        p = _masked_softmax_rows(s, mask)
        p_new = jnp.sum(jnp.where(lane == slot, p, 0.0), axis=-1, keepdims=True)
        p_old = jnp.where(lane == slot, 0.0, p)
        return _dot_p(p_old, vals) + p_new * v_new

    for k in range(KV_HEADS):
        o_k = attend(kvbuf[k, :, k, :], kvbuf[k, :, KV_HEADS + k, :], new_ref[2 * KV_HEADS + k:2 * KV_HEADS + k + 1, :],
                     new_ref[3 * KV_HEADS + k:3 * KV_HEADS + k + 1, :], lane_s, new_col, lane_s <= new_col)
        o_s = o_s + jnp.where(grp == k, o_k, 0.0)
        o_k = attend(win_ref[:, k, :], win_ref[:, KV_HEADS + k, :],
                     new_ref[4 * KV_HEADS + k:4 * KV_HEADS + k + 1, :],
                     new_ref[5 * KV_HEADS + k:5 * KV_HEADS + k + 1, :], lane_w, 0, lane_w >= 0)
        o_w = o_w + jnp.where(grp == k, o_k, 0.0)

    sig = jnp.broadcast_to(_sigmoid(gn_ref[...]), (N_HEADS, LANES))
    lane = lax.broadcasted_iota(i32, (N_HEADS, LANES), 1)
    head = lax.broadcasted_iota(i32, (N_HEADS, LANES), 0)
    out = None
    for br, o_br in enumerate((oc_ref[...], o_s, o_w)):
        gate = jnp.sum(jnp.where(lane == head * N_NSA_BRANCH + br, sig, 0.0), axis=-1, keepdims=True)
        term = gate * o_br
        out = term if out is None else out + term
    o_ref[...] = out.astype(o_ref.dtype)


def nsa_decode(tok_off, cache_tok, q3, o_c, gn3, new3, state_win5, layer, n_gather):
    B = q3.shape[0]
    n_win = state_win5.shape[2]
    n_keys = (n_gather + 1) * SEL_LEN
    kern = functools.partial(_nsa_decode_kernel, n_gather=n_gather)
    grid_spec = pltpu.PrefetchScalarGridSpec(
        num_scalar_prefetch=1,
        grid=(B,),
        in_specs=[pl.BlockSpec(memory_space=pl.ANY),
                  pl.BlockSpec((None, N_HEADS, HEAD_DIM), lambda b, t: (b, 0, 0)),
                  pl.BlockSpec((None, N_HEADS, HEAD_DIM), lambda b, t: (b, 0, 0)),
                  pl.BlockSpec((None, 1, LANES), lambda b, t: (b, 0, 0)),
                  pl.BlockSpec((None, 6 * KV_HEADS, HEAD_DIM), lambda b, t: (b, 0, 0)),
                  pl.BlockSpec((None, None, n_win, 2 * KV_HEADS, HEAD_DIM), lambda b, t: (layer, b, 0, 0, 0))],
        out_specs=pl.BlockSpec((None, N_HEADS, HEAD_DIM), lambda b, t: (b, 0, 0)),
        scratch_shapes=[pltpu.VMEM((KV_HEADS, n_keys, 2 * KV_HEADS, HEAD_DIM), f32),
                        pltpu.SemaphoreType.DMA((KV_HEADS,))],
    )
    return pl.pallas_call(
        kern,
        grid_spec=grid_spec,
        out_shape=jax.ShapeDtypeStruct((B, N_HEADS, HEAD_DIM), f32),
        compiler_params=_cparams(("arbitrary",)),
        name="nsa_decode",
    )(tok_off, cache_tok, q3, o_c, gn3, new3, state_win5)


def _rope_tables(pos):
    half = HEAD_DIM // 2
    inv = jnp.power(ROPE_THETA, -jnp.arange(0, HEAD_DIM, 2, dtype=f32) / HEAD_DIM)
    ang = pos.astype(f32)[:, None] * inv[None, :]
    cos, sin = jnp.cos(ang), jnp.sin(ang)
    return jnp.concatenate([cos, cos], axis=-1), jnp.concatenate([-sin, sin], axis=-1)


def _trunk(x, t0, past, p):
    B, T, D = x.shape
    M = B * T
    depth = p["w_in"].shape[0]
    d_pool = p["pool_scale"].shape[-1]
    d_main = d_pool + D_ATTN + 6 * D_KV
    gate_col = d_main
    mrg_col = d_main + N_HEADS * N_NSA_BRANCH
    tn_main = 512
    assert d_pool % tn_main == 0 and d_main % tn_main == 0 and gate_col % LANES == 0
    q_blk0 = d_pool // tn_main
    kv_blk0 = (d_pool + D_ATTN) // tn_main
    rope_blocks = tuple(range(q_blk0, kv_blk0)) + tuple(kv_blk0 + r for r in (0, 2, 4))
    kv_col = (d_pool + D_ATTN) // HEAD_DIM

    cos, sin = _rope_tables(t0 + jnp.arange(T))
    tm = _pick(M, 1024, 8)
    if T >= tm:
        n_tab = T // tm
    else:
        cos, sin = jnp.tile(cos, (tm // T, 1)), jnp.tile(sin, (tm // T, 1))
        n_tab = 1

    def rope_specs(tm_, tn_):
        assert tm_ == tm
        spec = pl.BlockSpec((tm, HEAD_DIM), lambda i, j, k: (i % n_tab, 0))
        return [spec, spec]

    x2 = x.reshape(M, D)
    w_mrg = p["w_in"][:, :, mrg_col:]
    wc = jnp.concatenate([p["cmp_w1"][:, :, :CMP_STRIDE], p["cmp_w1"][:, :, CMP_STRIDE:]], axis=-1)
    cb1 = p["cmp_b1"][:, :, None, :]
    if past is not None:
        cache_kv, page_table, state_win, state_pool = past
        n_pool = cache_kv.shape[1]
        n_pages = page_table.shape[1]
        past_len = n_pages * PAGE_SIZE
        assert T == 1 and t0 == past_len and state_win.shape[2] == WINDOW
        n_blk = past_len // SEL_LEN + 1
        assert n_blk >= N_SEL
        nbp = -(-n_blk // LANES) * LANES
        cache_pages = cache_kv.reshape(depth * n_pool, PAGE_SIZE, 4 * KV_HEADS, HEAD_DIM)
        cache_tok = cache_kv.reshape(depth * n_pool * PAGE_SIZE, 4 * KV_HEADS, HEAD_DIM)
        state_win5 = state_win.reshape(depth, B, WINDOW, 2 * KV_HEADS, HEAD_DIM)
        ov_s = jnp.repeat(_cmp_to_sel(past_len // CMP_STRIDE, n_blk, nbp), KV_HEADS, axis=0)

    precise = past is not None
    act = f32 if precise else bf16
    kv_new, win_new, pool_new = [], [], []
    for l in range(depth):
        xn = rmsnorm(x2, p["norm_mix"], l, act)
        proj = _matmul(xn, [p["w_in"]], (l,), d_main, 0, [cos, sin], rope_specs,
                       functools.partial(_ep_inproj, rope_blocks=rope_blocks), f32,
                       tm_pref=tm, tn_pref=tn_main, name="in_proj", precise=precise)
        gn = _matmul(xn, [p["w_in"]], (l,), LANES, gate_col // LANES, [], _no_extras, _ep_plain, f32,
                     tm_pref=tm, tn_pref=LANES, name="gate_proj", precise=precise)
        g_mrg = _matmul(xn, [w_mrg], (l,), 2 * D, 0, [], _no_extras, _ep_plain, f32, tm_pref=tm,
                        name="merge_gate_proj", precise=precise)

        rows = proj[:, d_pool + D_ATTN:d_pool + D_ATTN + 4 * D_KV].reshape(B, T, 4, KV_HEADS, HEAD_DIM)
        wrows = proj[:, d_pool + D_ATTN + 4 * D_KV:d_main].reshape(B, T, 2, KV_HEADS, HEAD_DIM)
        u = proj[:, :d_pool].reshape(B, T, d_pool)
        if past is None:
            proj3 = proj.reshape(B, T, d_main)
            pre16 = jnp.zeros((B, POOL_KEEP + 1, d_pool), f32)
            pool_out = pool_prompt(proj3, pre16, p["pool_w"], p["pool_scale"], l, t0).reshape(M, d_pool)
            kc = compress_prompt(proj3, wc, cb1, p["cmp_w2"], l, kv_col)
            attn = nsa_prompt(proj3, gn.reshape(B, T, LANES), kc, q_blk0 // (GROUP * HEAD_DIM // tn_main), kv_col)
            attn = attn.reshape(M, D_ATTN)
            win_new.append(wrows[:, -min(WINDOW, T):])
            pool_new.append(u[:, -POOL_KEEP:])
        else:
            state_t = jnp.transpose(state_pool[l], (1, 0, 2))
            pool_out = pool_sample(proj, state_t, p["pool_w"], p["pool_scale"], l, t0)
            hbuf = compress_pages(cache_pages, page_table, wc, l, n_pool)
            kc = compress_finish(hbuf, cb1, p["cmp_w2"], l)
            q3 = proj[:, d_pool:d_pool + D_ATTN].reshape(B, N_HEADS, HEAD_DIM)
            o_c, sel_idx = nsa_select(q3, kc, ov_s, t0, n_blk)
            n_gather = N_SEL - 1
            blk_idx = sel_idx[:, :, :n_gather]
            per_page = PAGE_SIZE // SEL_LEN
            page = jnp.take_along_axis(page_table, blk_idx.reshape(B, -1) // per_page, axis=1)
            tok_off = ((l * n_pool + page) * PAGE_SIZE + (blk_idx.reshape(B, -1) % per_page) * SEL_LEN)
            new3 = proj[:, d_pool + D_ATTN:d_main].reshape(B, 6 * KV_HEADS, HEAD_DIM)
            attn = nsa_decode(tok_off.reshape(-1).astype(i32), cache_tok, q3, o_c, gn.reshape(B, 1, LANES),
                              new3, state_win5, l, n_gather).reshape(M, D_ATTN)
            win_new.append(jnp.concatenate([state_win[l], wrows], axis=1)[:, -WINDOW:])
            pool_new.append(jnp.concatenate([state_pool[l], u], axis=1)[:, -POOL_KEEP:])
        kv_new.append(rows)

        pp = _matmul(pool_out, [p["w_pool_proj"]], (l,), D, 0, [], _no_extras, _ep_plain, f32, tm_pref=tm,
                     name="pool_proj", precise=precise)

        def merge_specs(tm_, tn_):
            nb = D // tn_
            return [pl.BlockSpec((tm_, tn_), lambda i, j, k: (i, j)),
                    pl.BlockSpec((tm_, tn_), lambda i, j, k: (i, j)),
                    pl.BlockSpec((tm_, tn_), lambda i, j, k: (i, j + nb))]

        h = _matmul(attn, [p["w_attn_proj"]], (l,), D, 0, [pp, g_mrg, g_mrg], merge_specs, _ep_merge, act,
                    tm_pref=tm, name="attn_proj_merge", precise=precise)

        def res_specs(tm_, tn_):
            return [pl.BlockSpec((tm_, tn_), lambda i, j, k: (i, j))]

        x2 = _matmul(h, [p["w_out"]], (l,), D, 0, [x2], res_specs, _ep_residual, f32, tm_pref=tm,
                     name="out_proj", precise=precise)

        xn = rmsnorm(x2, p["norm_ffn"], l, act if l % 2 == 0 else f32)
        if l % 2 == 0:
            hid = _matmul(xn, [p["ffn_wg"], p["ffn_wu"]], (l // 2,), p["ffn_wg"].shape[-1], 0, [], _no_extras,
                          _ep_swiglu, act, tm_pref=tm, name="ffn_up", precise=precise)
            x2 = _matmul(hid, [p["ffn_wd"]], (l // 2,), D, 0, [x2], res_specs, _ep_residual, f32,
                         tm_pref=tm, name="ffn_down", precise=precise)
        else:
            w_r = jnp.pad(p["moe_router"], ((0, 0), (0, 0), (0, LANES - N_EXPERTS)))
            comb = _matmul(xn, [w_r], (l // 2,), LANES, 0, [], _no_extras, _ep_router, f32, tm_pref=tm,
                           name="router", precise=precise)
            out = None
            if 2 * M >= N_EXPERTS * tm:
                gates, pos, src_tok, tile_expert, n_used = _route(comb, tm)
                xs = moe_dispatch(xn, src_tok, n_used, tm)
                hid = _matmul(xs, [p["moe_wg"], p["moe_wu"]], (l // 2,), p["moe_wg"].shape[-1], 0, [], _no_extras,
                              _ep_swiglu, bf16, tm_pref=tm, name="moe_up_sorted", group=(tile_expert, n_used))
                ys = _matmul(hid, [p["moe_wd"]], (l // 2,), D, 0, [], _no_extras, _ep_plain, f32,
                             tm_pref=tm, name="moe_down_sorted", group=(tile_expert, n_used))
                out = moe_combine(ys, pos, jnp.pad(gates, ((0, 0), (0, LANES - 2))), x2)
            for e in range(N_EXPERTS if out is None else 0):
                hid = _matmul(xn, [p["moe_wg"], p["moe_wu"]], (l // 2, e), p["moe_wg"].shape[-1], 0, [], _no_extras,
                              _ep_swiglu, bf16, tm_pref=tm, name="moe_up")
                first, last = e == 0, e == N_EXPERTS - 1
                extras = [comb] + ([] if first else [out]) + ([x2] if last else [])

                def moe_specs(tm_, tn_, n=len(extras)):
                    return ([pl.BlockSpec((tm_, LANES), lambda i, j, k: (i, 0))]
                            + [pl.BlockSpec((tm_, tn_), lambda i, j, k: (i, j))] * (n - 1))

                out = _matmul(hid, [p["moe_wd"]], (l // 2, e), D, 0, extras, moe_specs,
                              functools.partial(_ep_moe_acc, expert=e, first=first, last=last), f32,
                              tm_pref=tm, name="moe_down")
            x2 = out

    y = rmsnorm(x2, p["norm_final"][None], 0, f32).reshape(B, T, D)
    return y, jnp.stack(kv_new, 0), jnp.stack(win_new, 0), jnp.stack(pool_new, 0)


def kernel(x_prompt, x_sample, cache_kv, state_win, state_pool, page_table, norm_mix, w_in, pool_w, pool_scale,
           cmp_w1, cmp_b1, cmp_w2, w_pool_proj, w_attn_proj, w_out, norm_ffn, ffn_wg, ffn_wu, ffn_wd,
           moe_router, moe_wg, moe_wu, moe_wd, norm_final):
    p = dict(norm_mix=norm_mix, w_in=w_in, pool_w=pool_w, pool_scale=pool_scale, cmp_w1=cmp_w1, cmp_b1=cmp_b1,
             cmp_w2=cmp_w2, w_pool_proj=w_pool_proj, w_attn_proj=w_attn_proj, w_out=w_out, norm_ffn=norm_ffn,
             ffn_wg=ffn_wg, ffn_wu=ffn_wu, ffn_wd=ffn_wd, moe_router=moe_router, moe_wg=moe_wg, moe_wu=moe_wu,
             moe_wd=moe_wd, norm_final=norm_final)
    y_p, kv_p, win_p, pool_p = _trunk(x_prompt, 0, None, p)
    past_len = page_table.shape[1] * PAGE_SIZE
    y_s, kv_s, win_s, pool_s = _trunk(x_sample, past_len, (cache_kv, page_table, state_win, state_pool), p)
    return (y_p, y_s, kv_p, kv_s, win_p, win_s, pool_p, pool_s)
```

```python
import functools
import math

import jax
import jax.numpy as jnp
from jax import lax
from jax.experimental import pallas as pl
from jax.experimental.pallas import tpu as pltpu

f32 = jnp.float32
bf16 = jnp.bfloat16
i32 = jnp.int32

N_HEADS = 16
HEAD_DIM = 128
KV_HEADS = 4
GROUP = N_HEADS // KV_HEADS
CMP_LEN = 32
CMP_STRIDE = 16
SEL_LEN = 64
N_SEL = 16
WINDOW = 512
N_NSA_BRANCH = 3
FORCE = 1e4
ROPE_THETA = 10000.0
POOL_WINDOWS = (2, 4, 8, 16)
POOL_KEEP = max(POOL_WINDOWS) - 1
N_EXPERTS = 8
EPS = 1e-6
NEG = -1e30
SCALE = HEAD_DIM ** -0.5
PAGE_SIZE = 128
LANES = 128
D_KV = KV_HEADS * HEAD_DIM
D_ATTN = N_HEADS * HEAD_DIM
VMEM_LIMIT = 56 * 1024 * 1024

_NT = (((1,), (1,)), ((), ()))


def _cparams(sem):
    return pltpu.CompilerParams(dimension_semantics=sem, vmem_limit_bytes=VMEM_LIMIT)


def _pick(dim, pref, mult=LANES):
    if dim <= pref:
        return dim
    t = (pref // mult) * mult
    while t >= mult:
        if dim % t == 0:
            return t
        t -= mult
    return dim


def _sigmoid(x):
    return 1.0 / (1.0 + jnp.exp(-x))


def _gelu_tanh(x):
    c = math.sqrt(2.0 / math.pi)
    return x * (0.5 * (1.0 + jnp.tanh(c * (x + 0.044715 * (x * x * x)))))


def _rmsnorm_kernel(x_ref, g_ref, o_ref):
    x = x_ref[...]
    ms = jnp.mean(x * x, axis=-1, keepdims=True)
    o_ref[...] = ((x * lax.rsqrt(ms + EPS)) * g_ref[...]).astype(o_ref.dtype)


def rmsnorm(x, gains, layer, out_dtype):
    M, D = x.shape
    tm = _pick(M, 512, 8)
    g3 = gains.reshape(gains.shape[0], 1, D)
    return pl.pallas_call(
        _rmsnorm_kernel,
        grid=(M // tm,),
        in_specs=[pl.BlockSpec((tm, D), lambda i: (i, 0)),
                  pl.BlockSpec((None, 1, D), lambda i: (layer, 0, 0))],
        out_specs=pl.BlockSpec((tm, D), lambda i: (i, 0)),
        out_shape=jax.ShapeDtypeStruct((M, D), out_dtype),
        compiler_params=_cparams(("parallel",)),
        name="rmsnorm",
    )(x, g3)


def _split(a):
    hi = a.astype(bf16)
    return hi, (a - hi.astype(f32)).astype(bf16)


def _dot_p(a, b, dims=None):
    dims = dims or (((a.ndim - 1,), (0,)), ((), ()))
    a_parts = _split(a) if a.dtype == f32 else (a,)
    b_parts = _split(b) if b.dtype == f32 else (b,)
    out = None
    for ia, ap in enumerate(a_parts):
        for ib, bp in enumerate(b_parts):
            if ia + ib < 2:
                t = lax.dot_general(ap, bp, dims, preferred_element_type=f32)
                out = t if out is None else out + t
    return out


def _mm_body(x_ref, w_refs, extras, o_ref, acc_refs, nk, epilogue, precise):
    j = pl.program_id(1)
    k = pl.program_id(2)
    if precise:
        parts = [_dot_p(x_ref[...], w_ref[...]) for w_ref in w_refs]
    else:
        x = x_ref[...].astype(bf16)
        parts = [jnp.dot(x, w_ref[...].astype(bf16), preferred_element_type=f32) for w_ref in w_refs]
    if nk == 1:
        epilogue(parts, extras, o_ref, j)
        return

    @pl.when(k == 0)
    def _():
        for a, p in zip(acc_refs, parts):
            a[...] = p

    @pl.when(k > 0)
    def _():
        for a, p in zip(acc_refs, parts):
            a[...] += p

    @pl.when(k == nk - 1)
    def _():
        epilogue([a[...] for a in acc_refs], extras, o_ref, j)


def _mm_kernel(*refs, n_w, n_extra, nk, epilogue, grouped, precise):
    if grouped:
        _, n_used_ref = refs[:2]
        refs = refs[2:]
    x_ref = refs[0]
    w_refs = refs[1:1 + n_w]
    extras = refs[1 + n_w:1 + n_w + n_extra]
    o_ref = refs[1 + n_w + n_extra]
    acc_refs = refs[2 + n_w + n_extra:]
    if not grouped:
        _mm_body(x_ref, w_refs, extras, o_ref, acc_refs, nk, epilogue, precise)
        return
    used = pl.program_id(0) < n_used_ref[0]

    @pl.when(used)
    def _():
        _mm_body(x_ref, w_refs, extras, o_ref, acc_refs, nk, epilogue, precise)

    @pl.when(jnp.logical_not(used))
    def _():
        o_ref[...] = jnp.zeros(o_ref.shape, o_ref.dtype)


def _matmul(x, ws, w_lead, n_cols, col_off_blocks, extras, extra_specs, epilogue, out_dtype,
            tm_pref=1024, tn_pref=512, tk_pref=2048, name="matmul", group=None, precise=False):
    M, K = x.shape
    tm = _pick(M, tm_pref, 8)
    tn = _pick(n_cols, tn_pref)
    tk = _pick(K, tk_pref)
    nk = K // tk
    grouped = group is not None
    n_lead = len(w_lead) + (1 if grouped else 0)

    def w_index(i, j, k, *pref):
        lead = tuple(w_lead) + ((pref[0][i],) if grouped else ())
        return lead + (k, j + col_off_blocks)

    w_spec = pl.BlockSpec((None,) * n_lead + (tk, tn), w_index)
    kern = functools.partial(_mm_kernel, n_w=len(ws), n_extra=len(extras), nk=nk, epilogue=epilogue,
                             grouped=grouped, precise=precise)
    scratch = [pltpu.VMEM((tm, tn), f32) for _ in ws] if nk > 1 else []
    grid_spec = pltpu.PrefetchScalarGridSpec(
        num_scalar_prefetch=2 if grouped else 0,
        grid=(M // tm, n_cols // tn, nk),
        in_specs=[pl.BlockSpec((tm, tk), lambda i, j, k, *_: (i, k))] + [w_spec] * len(ws) + list(extra_specs(tm, tn)),
        out_specs=pl.BlockSpec((tm, tn), lambda i, j, k, *_: (i, j)),
        scratch_shapes=scratch,
    )
    return pl.pallas_call(
        kern,
        grid_spec=grid_spec,
        out_shape=jax.ShapeDtypeStruct((M, n_cols), out_dtype),
        compiler_params=_cparams(("parallel", "parallel", "arbitrary")),
        name=name,
    )(*(group or ()), x, *ws, *extras)


def _no_extras(tm, tn):
    return []


def _ep_plain(accs, extras, o_ref, j):
    o_ref[...] = accs[0].astype(o_ref.dtype)


def _ep_residual(accs, extras, o_ref, j):
    o_ref[...] = (extras[0][...] + accs[0]).astype(o_ref.dtype)


def _ep_swiglu(accs, extras, o_ref, j):
    g, u = accs
    o_ref[...] = ((g * _sigmoid(g)) * u).astype(o_ref.dtype)


def _ep_merge(accs, extras, o_ref, j):
    p_ref, g0_ref, g1_ref = extras
    h = _sigmoid(g0_ref[...]) * p_ref[...] + _sigmoid(g1_ref[...]) * accs[0]
    o_ref[...] = h.astype(o_ref.dtype)


def _rope_tile(v, cos, sin):
    outs = []
    for h in range(v.shape[1] // HEAD_DIM):
        seg = v[:, h * HEAD_DIM:(h + 1) * HEAD_DIM]
        outs.append(seg * cos + pltpu.roll(seg, HEAD_DIM // 2, axis=1) * sin)
    return jnp.concatenate(outs, axis=1) if len(outs) > 1 else outs[0]


def _ep_inproj(accs, extras, o_ref, j, *, rope_blocks):
    cos_ref, sin_ref = extras
    flag = functools.reduce(jnp.logical_or, [j == b for b in rope_blocks])

    @pl.when(flag)
    def _():
        o_ref[...] = _rope_tile(accs[0], cos_ref[...], sin_ref[...]).astype(o_ref.dtype)

    @pl.when(jnp.logical_not(flag))
    def _():
        o_ref[...] = accs[0].astype(o_ref.dtype)


def _ep_router(accs, extras, o_ref, j):
    logits = accs[0]
    lane = lax.broadcasted_iota(i32, logits.shape, 1)
    lg = jnp.where(lane < N_EXPERTS, logits, -jnp.inf)
    m1 = jnp.max(lg, axis=-1, keepdims=True)
    i1 = jnp.min(jnp.where(lg == m1, lane, LANES), axis=-1, keepdims=True)
    lg2 = jnp.where(lane == i1, -jnp.inf, lg)
    m2 = jnp.max(lg2, axis=-1, keepdims=True)
    i2 = jnp.min(jnp.where(lg2 == m2, lane, LANES), axis=-1, keepdims=True)
    e2 = jnp.exp(m2 - m1)
    den = 1.0 + e2
    comb = jnp.where(lane == i1, 1.0 / den, 0.0) + jnp.where(lane == i2, e2 / den, 0.0)
    o_ref[...] = comb.astype(o_ref.dtype)


def _ep_moe_acc(accs, extras, o_ref, j, *, expert, first, last):
    lane_src = extras[0][...]
    lane = lax.broadcasted_iota(i32, lane_src.shape, 1)
    c = jnp.sum(jnp.where(lane == expert, lane_src, 0.0), axis=-1, keepdims=True)
    out = c * accs[0]
    nxt = 1
    if not first:
        out = extras[nxt][...] + out
        nxt += 1
    if last:
        out = extras[nxt][...] + out
    o_ref[...] = out.astype(o_ref.dtype)


def _route(comb, tm):
    M = comb.shape[0]
    gates, idx = lax.top_k(comb[:, :N_EXPERTS], 2)
    e_flat = idx.reshape(-1).astype(i32)
    onehot = (e_flat[:, None] == jnp.arange(N_EXPERTS, dtype=i32)[None, :]).astype(i32)
    csum = jnp.cumsum(onehot, axis=0)
    rank = jnp.sum((csum - onehot) * onehot, axis=1)
    tiles_e = (csum[-1] + tm - 1) // tm
    tile_end = jnp.cumsum(tiles_e)
    pos = (tile_end - tiles_e)[e_flat] * tm + rank
    n_tiles = 2 * M // tm + N_EXPERTS
    src_tok = (jnp.arange(n_tiles * tm, dtype=i32) % M).at[pos].set(jnp.arange(2 * M, dtype=i32) // 2)
    n_used = tile_end[-1]
    t = jnp.minimum(jnp.arange(n_tiles, dtype=i32), n_used - 1)
    tile_expert = jnp.minimum(jnp.searchsorted(tile_end, t, side="right"), N_EXPERTS - 1).astype(i32)
    return gates, pos.reshape(M, 2).astype(i32), src_tok, tile_expert, n_used.reshape(1).astype(i32)


def _row_copy(src_hbm, row, dst, r, sem):
    return pltpu.make_async_copy(src_hbm.at[pl.ds(row, 1), :], dst.at[pl.ds(r, 1), :], sem)


def _dispatch_kernel(src_ref, used_ref, x_ref, o_ref, buf, sem, *, rows, tm):
    base = pl.program_id(0) * rows
    used = base < used_ref[0] * tm

    def start(r, c):
        _row_copy(x_ref, src_ref[base + r], buf, r, sem.at[0]).start()
        return c

    def wait(r, c):
        _row_copy(x_ref, src_ref[base + r], buf, r, sem.at[0]).wait()
        return c

    @pl.when(used)
    def _():
        lax.fori_loop(0, rows, start, 0, unroll=8)
        lax.fori_loop(0, rows, wait, 0, unroll=8)
        o_ref[...] = buf[...].astype(o_ref.dtype)

    @pl.when(jnp.logical_not(used))
    def _():
        o_ref[...] = jnp.zeros(o_ref.shape, o_ref.dtype)


def moe_dispatch(x, src_tok, n_used, tm):
    P = src_tok.shape[0]
    D = x.shape[1]
    rows = _pick(tm, 256, 8)
    grid_spec = pltpu.PrefetchScalarGridSpec(
        num_scalar_prefetch=2,
        grid=(P // rows,),
        in_specs=[pl.BlockSpec(memory_space=pl.ANY)],
        out_specs=pl.BlockSpec((rows, D), lambda i, s, u: (i, 0)),
        scratch_shapes=[pltpu.VMEM((rows, D), f32), pltpu.SemaphoreType.DMA((1,))],
    )
    return pl.pallas_call(
        functools.partial(_dispatch_kernel, rows=rows, tm=tm),
        grid_spec=grid_spec,
        out_shape=jax.ShapeDtypeStruct((P, D), bf16),
        compiler_params=_cparams(("arbitrary",)),
        name="moe_dispatch",
    )(src_tok, n_used, x)


def _combine_kernel(pos_ref, y_ref, x_ref, g_ref, o_ref, buf, sem, *, rows):
    base = pl.program_id(0) * rows

    def start(r, c):
        for j in range(2):
            _row_copy(y_ref, pos_ref[2 * (base + r) + j], buf.at[j], r, sem.at[j]).start()
        return c

    def wait(r, c):
        for j in range(2):
            _row_copy(y_ref, pos_ref[2 * (base + r) + j], buf.at[j], r, sem.at[j]).wait()
        return c

    lax.fori_loop(0, rows, start, 0, unroll=4)
    lax.fori_loop(0, rows, wait, 0, unroll=4)
    g = g_ref[...]
    lane = lax.broadcasted_iota(i32, g.shape, 1)
    g0 = jnp.sum(jnp.where(lane == 0, g, 0.0), axis=-1, keepdims=True)
    g1 = jnp.sum(jnp.where(lane == 1, g, 0.0), axis=-1, keepdims=True)
    o_ref[...] = x_ref[...] + (g0 * buf[0] + g1 * buf[1])


def moe_combine(y_sorted, pos, gates_pad, x):
    M, D = x.shape
    rows = _pick(M, 256, 8)
    grid_spec = pltpu.PrefetchScalarGridSpec(
        num_scalar_prefetch=1,
        grid=(M // rows,),
        in_specs=[pl.BlockSpec(memory_space=pl.ANY),
                  pl.BlockSpec((rows, D), lambda i, s: (i, 0)),
                  pl.BlockSpec((rows, LANES), lambda i, s: (i, 0))],
        out_specs=pl.BlockSpec((rows, D), lambda i, s: (i, 0)),
        scratch_shapes=[pltpu.VMEM((2, rows, D), f32), pltpu.SemaphoreType.DMA((2,))],
    )
    return pl.pallas_call(
        functools.partial(_combine_kernel, rows=rows),
        grid_spec=grid_spec,
        out_shape=jax.ShapeDtypeStruct((M, D), f32),
        compiler_params=_cparams(("arbitrary",)),
        name="moe_combine",
    )(pos.reshape(-1), y_sorted, x, gates_pad)


def _pool_groups(shifted, u, pos1, pw_ref, sc_ref, o_ref, precise=False):
    pg = u.shape[1] // len(POOL_WINDOWS)
    for g, w in enumerate(POOL_WINDOWS):
        c = slice(g * pg, (g + 1) * pg)
        s = u[:, c]
        for jj in range(1, w):
            s = s + shifted(jj, c)
        cnt = jnp.minimum(float(w), pos1)
        pooled = s / cnt - u[:, c]
        if precise:
            mixed = _dot_p(pooled, pw_ref[g])
        else:
            mixed = jnp.dot(pooled.astype(bf16), pw_ref[g].astype(bf16), preferred_element_type=f32)
        o_ref[:, c] = (mixed * sc_ref[:, c]).astype(o_ref.dtype)


def _pool_prompt_kernel(u_ref, pre_ref, pw_ref, sc_ref, o_ref, ext_ref, *, tp, t0):
    i = pl.program_id(1)
    halo = POOL_KEEP + 1

    @pl.when(i == 0)
    def _():
        ext_ref[0:halo, :] = pre_ref[...]

    u = u_ref[...]
    ext_ref[halo:halo + tp, :] = u
    pos1 = (lax.broadcasted_iota(i32, (tp, 1), 0) + (i * tp + t0 + 1)).astype(f32)
    _pool_groups(lambda jj, c: ext_ref[halo - jj:halo - jj + tp, c], u, pos1, pw_ref, sc_ref, o_ref)
    ext_ref[0:halo, :] = ext_ref[tp:tp + halo, :]


def pool_prompt(proj3, prefix16, pool_w, pool_scale, layer, t0):
    B, T, _ = proj3.shape
    dp = pool_scale.shape[-1]
    tp = _pick(T, 512, 16)
    kern = functools.partial(_pool_prompt_kernel, tp=tp, t0=t0)
    sc3 = pool_scale.reshape(pool_scale.shape[0], 1, dp)
    return pl.pallas_call(
        kern,
        grid=(B, T // tp),
        in_specs=[pl.BlockSpec((None, tp, dp), lambda b, i: (b, i, 0)),
                  pl.BlockSpec((None, POOL_KEEP + 1, dp), lambda b, i: (b, 0, 0)),
                  pl.BlockSpec((None,) + pool_w.shape[1:], lambda b, i: (layer, 0, 0, 0)),
                  pl.BlockSpec((None, 1, dp), lambda b, i: (layer, 0, 0))],
        out_specs=pl.BlockSpec((None, tp, dp), lambda b, i: (b, i, 0)),
        out_shape=jax.ShapeDtypeStruct((B, T, dp), bf16),
        scratch_shapes=[pltpu.VMEM((tp + POOL_KEEP + 1, dp), f32)],
        compiler_params=_cparams(("arbitrary", "arbitrary")),
        name="pool_prompt",
    )(proj3, prefix16, pool_w, sc3)


def _pool_sample_kernel(u_ref, st_ref, pw_ref, sc_ref, o_ref, *, t0):
    u = u_ref[...]
    pos1 = jnp.full((u.shape[0], 1), float(t0 + 1), f32)
    _pool_groups(lambda jj, c: st_ref[POOL_KEEP - jj, :, c], u, pos1, pw_ref, sc_ref, o_ref, precise=True)


def pool_sample(proj, state_t, pool_w, pool_scale, layer, t0):
    B = proj.shape[0]
    dp = pool_scale.shape[-1]
    kern = functools.partial(_pool_sample_kernel, t0=t0)
    sc3 = pool_scale.reshape(pool_scale.shape[0], 1, dp)
    return pl.pallas_call(
        kern,
        grid=(1,),
        in_specs=[pl.BlockSpec((B, dp), lambda i: (0, 0)),
                  pl.BlockSpec((POOL_KEEP, B, dp), lambda i: (0, 0, 0)),
                  pl.BlockSpec((None,) + pool_w.shape[1:], lambda i: (layer, 0, 0, 0)),
                  pl.BlockSpec((None, 1, dp), lambda i: (layer, 0, 0))],
        out_specs=pl.BlockSpec((B, dp), lambda i: (0, 0)),
        out_shape=jax.ShapeDtypeStruct((B, dp), f32),
        compiler_params=_cparams(("arbitrary",)),
        name="pool_sample",
    )(proj, state_t, pool_w, sc3)


def _compress_tail(h01, b1, w2, step=1, precise=False):
    n = h01.shape[0]
    hd = h01.shape[1] // 2
    hid = h01[:, :hd] + pltpu.roll(h01[:, hd:], n - step, axis=0) + b1
    if precise:
        return _dot_p(_gelu_tanh(hid), w2)
    return jnp.dot(_gelu_tanh(hid).astype(bf16), w2.astype(bf16), preferred_element_type=f32)


def _compress_prompt_kernel(x_ref, wc_ref, b1_ref, w2_ref, o_ref, *, n_sub):
    acc = None
    for l in range(CMP_STRIDE):
        xl = x_ref[pl.ds(l, n_sub, stride=CMP_STRIDE), :].astype(bf16)
        p = jnp.dot(xl, wc_ref[l].astype(bf16), preferred_element_type=f32)
        acc = p if acc is None else acc + p
    o_ref[...] = _compress_tail(acc, b1_ref[...], w2_ref[...]).astype(o_ref.dtype)


def compress_prompt(proj3, wc, b1, w2, layer, kv_col_block):
    B, T, _ = proj3.shape
    n_sub = T // CMP_STRIDE
    hid2 = wc.shape[-1]
    kern = functools.partial(_compress_prompt_kernel, n_sub=n_sub)
    return pl.pallas_call(
        kern,
        grid=(B, 2, KV_HEADS),
        in_specs=[pl.BlockSpec((None, T, HEAD_DIM), lambda b, r, h: (b, 0, kv_col_block + r * KV_HEADS + h)),
                  pl.BlockSpec((None, None, CMP_STRIDE, HEAD_DIM, hid2), lambda b, r, h: (layer, r, 0, 0, 0)),
                  pl.BlockSpec((None, None, 1, hid2 // 2), lambda b, r, h: (layer, r, 0, 0)),
                  pl.BlockSpec((None, None, hid2 // 2, HEAD_DIM), lambda b, r, h: (layer, r, 0, 0))],
        out_specs=pl.BlockSpec((None, None, None, n_sub, HEAD_DIM), lambda b, r, h: (b, r, h, 0, 0)),
        out_shape=jax.ShapeDtypeStruct((B, 2, KV_HEADS, n_sub, HEAD_DIM), bf16),
        compiler_params=_cparams(("parallel", "parallel", "parallel")),
        name="compress_prompt",
    )(proj3, wc, b1, w2)


def _compress_pages_kernel(pt_ref, *refs, pg):
    page_refs = refs[:pg]
    wc_ref = refs[pg]
    o_ref = refs[pg + 1]
    sub = PAGE_SIZE // CMP_STRIDE
    is_k = lax.broadcasted_iota(i32, (2 * KV_HEADS, HEAD_DIM), 0) < KV_HEADS
    acc = [None, None]
    for l2 in range(CMP_STRIDE // 2):
        cols = ([], [])
        for l in (2 * l2, 2 * l2 + 1):
            k_tiles, v_tiles = [], []
            for s in range(pg):
                tok = page_refs[s][pl.ds(l, sub, stride=CMP_STRIDE)]
                for m in range(0, sub, 2):
                    a, b = tok[m], tok[m + 1]
                    k_tiles.append(jnp.where(is_k, a, pltpu.roll(b, KV_HEADS, axis=0)))
                    v_tiles.append(jnp.where(is_k, pltpu.roll(a, KV_HEADS, axis=0), b))
            cols[0].append(jnp.concatenate(k_tiles, axis=0))
            cols[1].append(jnp.concatenate(v_tiles, axis=0))
        for r in range(2):
            x2 = jnp.concatenate(cols[r], axis=1)
            p = _dot_p(x2, wc_ref[r, l2])
            acc[r] = p if acc[r] is None else acc[r] + p
    for r in range(2):
        o_ref[r] = acc[r]


def compress_pages(cache_pages, page_table, wc, layer, n_pool):
    B, n_pages = page_table.shape
    pg = math.gcd(n_pages, 8)
    sub = PAGE_SIZE // CMP_STRIDE
    hid2 = wc.shape[-1]
    wc2 = wc.reshape(wc.shape[0], 2, CMP_STRIDE // 2, 2 * HEAD_DIM, hid2)
    kern = functools.partial(_compress_pages_kernel, pg=pg)

    def page_spec(s):
        return pl.BlockSpec((None, PAGE_SIZE, 2 * KV_HEADS, HEAD_DIM),
                            lambda b, p, pt: (layer * n_pool + pt[b, p * pg + s], 0, 0, 0))

    grid_spec = pltpu.PrefetchScalarGridSpec(
        num_scalar_prefetch=1,
        grid=(B, n_pages // pg),
        in_specs=[page_spec(s) for s in range(pg)]
        + [pl.BlockSpec((None, 2, CMP_STRIDE // 2, 2 * HEAD_DIM, hid2), lambda b, p, pt: (layer, 0, 0, 0, 0))],
        out_specs=pl.BlockSpec((None, 2, pg * sub * KV_HEADS, hid2), lambda b, p, pt: (b, 0, p, 0)),
    )
    return pl.pallas_call(
        kern,
        grid_spec=grid_spec,
        out_shape=jax.ShapeDtypeStruct((B, 2, n_pages * sub * KV_HEADS, hid2), f32),
        compiler_params=_cparams(("parallel", "parallel")),
        name="compress_pages",
    )(page_table, *([cache_pages] * pg), wc2)


def _compress_finish_kernel(h_ref, b1_ref, w2_ref, o_ref):
    o_ref[...] = _compress_tail(h_ref[...], b1_ref[...], w2_ref[...], KV_HEADS, precise=True).astype(o_ref.dtype)


def compress_finish(hbuf, b1, w2, layer):
    B, _, n_rows, hid2 = hbuf.shape
    return pl.pallas_call(
        _compress_finish_kernel,
        grid=(B, 2),
        in_specs=[pl.BlockSpec((None, None, n_rows, hid2), lambda b, r: (b, r, 0, 0)),
                  pl.BlockSpec((None, None, 1, hid2 // 2), lambda b, r: (layer, r, 0, 0)),
                  pl.BlockSpec((None, None, hid2 // 2, HEAD_DIM), lambda b, r: (layer, r, 0, 0))],
        out_specs=pl.BlockSpec((None, None, n_rows, HEAD_DIM), lambda b, r: (b, r, 0, 0)),
        out_shape=jax.ShapeDtypeStruct((B, 2, n_rows, HEAD_DIM), f32),
        compiler_params=_cparams(("parallel", "parallel")),
        name="compress_finish",
    )(hbuf, b1, w2)


def _masked_softmax(s, mask, axis):
    s = jnp.where(mask, s, NEG)
    m = jnp.max(s, axis=axis, keepdims=True)
    e = jnp.where(mask, jnp.exp(s - m), 0.0)
    l = jnp.sum(e, axis=axis, keepdims=True)
    return e * jnp.where(l > 0.0, 1.0 / l, 0.0)


def _masked_softmax_rows(s, mask):
    return _masked_softmax(s, mask, -1)


def _nsa_prompt_kernel(q_ref, gn_ref, kc_ref, vc_ref, ks_ref, vs_ref, kw_ref, vw_ref, ovt_ref, ext_ref,
                       o_ref, ksb, vst, kwb, vwt, vct, m_sc, l_sc, acc_sc, *, tq, n_blk, kts, wlen, seq):
    kh = pl.program_id(1)
    i = pl.program_id(2)
    t0 = i * tq
    cols = GROUP * tq

    @pl.when(i == 0)
    def _():
        ch = min(seq, 512)
        for c in range(seq // ch):
            sl = slice(c * ch, (c + 1) * ch)
            ksb[sl, :] = ks_ref[sl, :].astype(bf16)
            kwb[sl, :] = kw_ref[sl, :].astype(bf16)
            vst[:, sl] = vs_ref[sl, :].T.astype(bf16)
            vwt[:, sl] = vw_ref[sl, :].T.astype(bf16)
        vct[...] = vc_ref[...].astype(f32).T.astype(bf16)

    q_all = jnp.concatenate([q_ref[:, g * HEAD_DIM:(g + 1) * HEAD_DIM] for g in range(GROUP)], axis=0).astype(bf16)
    t_q = t0 + lax.broadcasted_iota(i32, (1, tq), 1)
    t_col = jnp.concatenate([t_q] * GROUP, axis=1)

    n_cp = kc_ref.shape[0]
    s = lax.dot_general(kc_ref[...], q_all, _NT, preferred_element_type=f32) * SCALE
    cmp_end = lax.broadcasted_iota(i32, (n_cp, 1), 0) * CMP_STRIDE + (CMP_LEN - 1)
    p_c = _masked_softmax(s, cmp_end <= t_col, 0).astype(bf16)
    o_c = jnp.dot(vct[...], p_c, preferred_element_type=f32)

    imp_g = jnp.dot(ovt_ref[...], p_c, preferred_element_type=f32)
    imp = imp_g[:, 0:tq]
    for g in range(1, GROUP):
        imp = imp + imp_g[:, g * tq:(g + 1) * tq]
    nbr = imp.shape[0]
    blk = lax.broadcasted_iota(i32, (nbr, 1), 0)
    cur = t_q // SEL_LEN
    valid = blk * SEL_LEN <= t_q
    forced = (blk == 0) | (blk == cur) | (blk == cur - 1)
    imp = jnp.where(valid, jnp.where(forced, FORCE, imp), -jnp.inf)
    rank = jnp.zeros(imp.shape, f32)
    for ii in range(n_blk):
        row = imp[ii:ii + 1, :]
        beats = (row > imp) | ((row == imp) & (blk > ii))
        rank = rank + jnp.where(beats, 1.0, 0.0)
    sel = jnp.where((rank < float(N_SEL)) & valid, 1.0, 0.0).astype(bf16)

    m_sc[...] = jnp.full(m_sc.shape, NEG, f32)
    l_sc[...] = jnp.zeros(l_sc.shape, f32)
    acc_sc[...] = jnp.zeros(acc_sc.shape, f32)

    def body(kt, carry):
        off = pl.multiple_of(kt * kts, kts)
        sc = lax.dot_general(ksb[pl.ds(off, kts), :], q_all, _NT, preferred_element_type=f32) * SCALE
        hit = jnp.dot(ext_ref[kt], sel, preferred_element_type=f32)
        kpos = off + lax.broadcasted_iota(i32, (kts, 1), 0)
        bias = jnp.where((hit > 0.5) & (kpos <= t_q), 0.0, NEG)
        sc = sc + jnp.concatenate([bias] * GROUP, axis=1)
        m_old = m_sc[...]
        m_new = jnp.maximum(m_old, jnp.max(sc, axis=0, keepdims=True))
        alpha = jnp.exp(m_old - m_new)
        e = jnp.exp(sc - m_new)
        l_sc[...] = alpha * l_sc[...] + jnp.sum(e, axis=0, keepdims=True)
        acc_sc[...] = alpha * acc_sc[...] + jnp.dot(vst[:, pl.ds(off, kts)], e.astype(bf16),
                                                    preferred_element_type=f32)
        m_sc[...] = m_new
        return carry

    lax.fori_loop(0, (t0 + tq - 1) // kts + 1, body, 0)
    l = l_sc[...]
    o_s = acc_sc[...] * jnp.where(l > 0.0, 1.0 / l, 0.0)

    woff = pl.multiple_of(jnp.clip(t0 + tq - wlen, 0, seq - wlen), tq)
    sc = lax.dot_general(kwb[pl.ds(woff, wlen), :], q_all, _NT, preferred_element_type=f32) * SCALE
    kpos = woff + lax.broadcasted_iota(i32, (wlen, 1), 0)
    bias = jnp.where((kpos <= t_q) & (kpos > t_q - WINDOW), 0.0, NEG)
    sc = sc + jnp.concatenate([bias] * GROUP, axis=1)
    e = jnp.exp(sc - jnp.max(sc, axis=0, keepdims=True))
    p_w = (e * (1.0 / jnp.sum(e, axis=0, keepdims=True))).astype(bf16)
    o_w = jnp.dot(vwt[:, pl.ds(woff, wlen)], p_w, preferred_element_type=f32)

    sig_t = _sigmoid(gn_ref[...]).T
    gcol = lax.broadcasted_iota(i32, sig_t.shape, 0)
    for g in range(GROUP):
        c = slice(g * tq, (g + 1) * tq)
        out = None
        for br, o_br in enumerate((o_c, o_s, o_w)):
            colid = kh * (GROUP * N_NSA_BRANCH) + g * N_NSA_BRANCH + br
            gate = jnp.sum(jnp.where(gcol == colid, sig_t, 0.0), axis=0, keepdims=True)
            term = gate * o_br[:, c]
            out = term if out is None else out + term
        o_ref[:, g * HEAD_DIM:(g + 1) * HEAD_DIM] = out.T.astype(o_ref.dtype)


def _cmp_to_sel(n_rows, n_blk, n_cols):
    i0 = jnp.arange(n_rows)[:, None] * CMP_STRIDE
    j0 = jnp.arange(n_cols)[None, :] * SEL_LEN
    ov = jnp.clip(jnp.minimum(i0 + CMP_LEN, j0 + SEL_LEN) - jnp.maximum(i0, j0), 0, None)
    ov = jnp.where(jnp.arange(n_cols)[None, :] < n_blk, ov, 0)
    return (ov.astype(f32) / CMP_LEN).astype(bf16)


def nsa_prompt(proj3, gn3, kc, q_col, kv_col):
    B, T, _ = proj3.shape
    assert T % SEL_LEN == 0
    tq = _pick(T, 128, 64)
    kts = _pick(T, 512, tq)
    wlen = min(WINDOW + tq, T)
    assert T % tq == 0 and WINDOW % tq == 0
    n_blk = T // SEL_LEN
    nbr = -(-n_blk // SEL_LEN) * SEL_LEN
    n_cp = T // CMP_STRIDE
    ovt = _cmp_to_sel(n_cp, n_blk, nbr).T
    tok_blk = jnp.arange(T) // SEL_LEN
    ext = (tok_blk[:, None] == jnp.arange(nbr)[None, :]).astype(bf16).reshape(T // kts, kts, nbr)
    cols = GROUP * tq
    kern = functools.partial(_nsa_prompt_kernel, tq=tq, n_blk=n_blk, kts=kts, wlen=wlen, seq=T)

    def kv_spec(r):
        return pl.BlockSpec((None, T, HEAD_DIM), lambda b, k, i: (b, 0, kv_col + r * KV_HEADS + k))

    def cmp_spec(r):
        return pl.BlockSpec((None, None, None, n_cp, HEAD_DIM), lambda b, k, i: (b, r, k, 0, 0))

    return pl.pallas_call(
        kern,
        grid=(B, KV_HEADS, T // tq),
        in_specs=[pl.BlockSpec((None, tq, GROUP * HEAD_DIM), lambda b, k, i: (b, i, q_col + k)),
                  pl.BlockSpec((None, tq, LANES), lambda b, k, i: (b, i, 0)),
                  cmp_spec(0), cmp_spec(1), kv_spec(2), kv_spec(3), kv_spec(4), kv_spec(5),
                  pl.BlockSpec((nbr, n_cp), lambda b, k, i: (0, 0)),
                  pl.BlockSpec((T // kts, kts, nbr), lambda b, k, i: (0, 0, 0))],
        out_specs=pl.BlockSpec((None, tq, GROUP * HEAD_DIM), lambda b, k, i: (b, i, k)),
        out_shape=jax.ShapeDtypeStruct((B, T, D_ATTN), bf16),
        scratch_shapes=[pltpu.VMEM((T, HEAD_DIM), bf16), pltpu.VMEM((HEAD_DIM, T), bf16),
                        pltpu.VMEM((T, HEAD_DIM), bf16), pltpu.VMEM((HEAD_DIM, T), bf16),
                        pltpu.VMEM((HEAD_DIM, n_cp), bf16),
                        pltpu.VMEM((1, cols), f32), pltpu.VMEM((1, cols), f32), pltpu.VMEM((HEAD_DIM, cols), f32)],
        compiler_params=_cparams(("parallel", "parallel", "arbitrary")),
        name="nsa_prompt",
    )(proj3, gn3, kc, kc, proj3, proj3, proj3, proj3, ovt, ext)


def _group_rows(n_rows):
    return lax.broadcasted_iota(i32, (n_rows, 1), 0) // GROUP


def _nsa_select_kernel(q_ref, kc_ref, vc_ref, ov_ref, oc_ref, sel_ref, *, t_pos, n_blk):
    q = q_ref[...]
    n_rows = kc_ref.shape[0]
    nbp = ov_ref.shape[1]
    grp = _group_rows(N_HEADS)
    crow = lax.broadcasted_iota(i32, (1, n_rows), 1)
    cmp_end = (crow // KV_HEADS) * CMP_STRIDE + (CMP_LEN - 1)
    s = _dot_p(q, kc_ref[...], _NT) * SCALE
    p = _masked_softmax_rows(s, (cmp_end <= t_pos) & (crow % KV_HEADS == grp))
    oc_ref[...] = _dot_p(p, vc_ref[...])
    imp_h = _dot_p(p, ov_ref[...])
    blk = lax.broadcasted_iota(i32, (1, nbp), 1)
    cur = t_pos // SEL_LEN
    valid = (blk * SEL_LEN <= t_pos) & (blk < n_blk)
    forced = (blk == 0) | (blk == cur) | (blk == cur - 1)
    ri = lax.broadcasted_iota(i32, (nbp, nbp), 0)
    ci = lax.broadcasted_iota(i32, (nbp, nbp), 1)
    slot = lax.broadcasted_iota(i32, (1, LANES), 1).astype(f32)
    row_id = lax.broadcasted_iota(i32, (nbp, 1), 0).astype(f32)
    for k in range(KV_HEADS):
        imp = jnp.sum(jnp.where(grp == k, imp_h, 0.0), axis=0, keepdims=True)
        imp = jnp.where(valid, jnp.where(forced, FORCE, imp), -jnp.inf)
        a = jnp.broadcast_to(imp, (nbp, nbp))
        at = jnp.broadcast_to(jnp.sum(jnp.where(ri == ci, a, 0.0), axis=1, keepdims=True), (nbp, nbp))
        beats = (at > a) | ((at == a) & (ri < ci))
        rank = jnp.sum(jnp.where(beats, 1.0, 0.0), axis=0, keepdims=True)
        sel = jnp.where((rank < float(N_SEL)) & valid & (blk != cur), 1.0, 0.0)
        sel_b = jnp.broadcast_to(sel, (nbp, nbp))
        sel_col = jnp.sum(jnp.where(ri == ci, sel_b, 0.0), axis=1, keepdims=True)
        before = jnp.sum(jnp.where(ci < ri, sel_b, 0.0), axis=1, keepdims=True)
        hit = (before == slot) & (sel_col > 0.5)
        sel_ref[k:k + 1, :] = jnp.sum(jnp.where(hit, row_id, 0.0), axis=0, keepdims=True).astype(i32)


def nsa_select(q3, kc, ov, t_pos, n_blk):
    B = q3.shape[0]
    n_rows = kc.shape[2]
    nbp = ov.shape[1]
    kern = functools.partial(_nsa_select_kernel, t_pos=t_pos, n_blk=n_blk)
    return pl.pallas_call(
        kern,
        grid=(B,),
        in_specs=[pl.BlockSpec((None, N_HEADS, HEAD_DIM), lambda b: (b, 0, 0)),
                  pl.BlockSpec((None, None, n_rows, HEAD_DIM), lambda b: (b, 0, 0, 0)),
                  pl.BlockSpec((None, None, n_rows, HEAD_DIM), lambda b: (b, 1, 0, 0)),
                  pl.BlockSpec((n_rows, nbp), lambda b: (0, 0))],
        out_specs=[pl.BlockSpec((None, N_HEADS, HEAD_DIM), lambda b: (b, 0, 0)),
                   pl.BlockSpec((None, KV_HEADS, LANES), lambda b: (b, 0, 0))],
        out_shape=[jax.ShapeDtypeStruct((B, N_HEADS, HEAD_DIM), f32),
                   jax.ShapeDtypeStruct((B, KV_HEADS, LANES), i32)],
        compiler_params=_cparams(("parallel",)),
        name="nsa_select",
    )(q3, kc, kc, ov)


def _gather_copies(tok_ref, cache_ref, kvbuf, sems, b, n_gather):
    copies = []
    for k in range(KV_HEADS):
        for s in range(n_gather):
            off = pl.multiple_of(tok_ref[(b * KV_HEADS + k) * n_gather + s], SEL_LEN)
            src = cache_ref.at[pl.ds(off, SEL_LEN), pl.ds(2 * KV_HEADS, 2 * KV_HEADS), :]
            copies.append(pltpu.make_async_copy(src, kvbuf.at[k, pl.ds(s * SEL_LEN, SEL_LEN)], sems.at[k]))
    return copies


def _nsa_decode_kernel(tok_ref, cache_ref, q_ref, oc_ref, gn_ref, new_ref, win_ref, o_ref, kvbuf, sems,
                       *, n_gather):
    b = pl.program_id(0)
    n_keys = (n_gather + 1) * SEL_LEN
    new_col = n_gather * SEL_LEN
    for k in range(KV_HEADS):
        kvbuf[k, new_col:n_keys] = jnp.zeros((SEL_LEN, 2 * KV_HEADS, HEAD_DIM), f32)
    copies = _gather_copies(tok_ref, cache_ref, kvbuf, sems, b, n_gather)
    for c in copies:
        c.start()
    for c in copies:
        c.wait()

    q = q_ref[...]
    grp = _group_rows(N_HEADS)
    lane_s = lax.broadcasted_iota(i32, (1, n_keys), 1)
    n_win = win_ref.shape[0]
    lane_w = lax.broadcasted_iota(i32, (1, n_win), 1)
    o_s = jnp.zeros((N_HEADS, HEAD_DIM), f32)
    o_w = jnp.zeros((N_HEADS, HEAD_DIM), f32)

    def attend(keys, vals, k_new, v_new, lane, slot, mask):
        qb = q.astype(bf16)
        s = lax.dot_general(qb, keys.astype(bf16), _NT, preferred_element_type=f32)
        s_new = jnp.sum(qb.astype(f32) * k_new.astype(bf16).astype(f32), axis=-1, keepdims=True)
        s = jnp.where(lane == slot, s_new, s) * SCALE
        p = _masked_softmax_rows(s, mask).astype(bf16)
        p_new = jnp.sum(jnp.where(lane == slot, p.astype(f32), 0.0), axis=-1, keepdims=True)
        p_old = jnp.where(lane == slot, 0.0, p.astype(f32)).astype(bf16)
        return (jnp.dot(p_old, vals.astype(bf16), preferred_element_type=f32)
                + p_new * v_new.astype(bf16).astype(f32))

    for k in range(KV_HEADS):
        o_k = attend(kvbuf[k, :, k, :], kvbuf[k, :, KV_HEADS + k, :], new_ref[2 * KV_HEADS + k:2 * KV_HEADS + k + 1, :],
                     new_ref[3 * KV_HEADS + k:3 * KV_HEADS + k + 1, :], lane_s, new_col, lane_s <= new_col)
        o_s = o_s + jnp.where(grp == k, o_k, 0.0)
        o_k = attend(win_ref[:, k, :], win_ref[:, KV_HEADS + k, :],
                     new_ref[4 * KV_HEADS + k:4 * KV_HEADS + k + 1, :],
                     new_ref[5 * KV_HEADS + k:5 * KV_HEADS + k + 1, :], lane_w, 0, lane_w >= 0)
        o_w = o_w + jnp.where(grp == k, o_k, 0.0)

    sig = jnp.broadcast_to(_sigmoid(gn_ref[...]), (N_HEADS, LANES))
    lane = lax.broadcasted_iota(i32, (N_HEADS, LANES), 1)
    head = lax.broadcasted_iota(i32, (N_HEADS, LANES), 0)
    out = None
    for br, o_br in enumerate((oc_ref[...], o_s, o_w)):
        gate = jnp.sum(jnp.where(lane == head * N_NSA_BRANCH + br, sig, 0.0), axis=-1, keepdims=True)
        term = gate * o_br
        out = term if out is None else out + term
    o_ref[...] = out.astype(o_ref.dtype)


def nsa_decode(tok_off, cache_tok, q3, o_c, gn3, new3, state_win5, layer, n_gather):
    B = q3.shape[0]
    n_win = state_win5.shape[2]
    n_keys = (n_gather + 1) * SEL_LEN
    kern = functools.partial(_nsa_decode_kernel, n_gather=n_gather)
    grid_spec = pltpu.PrefetchScalarGridSpec(
        num_scalar_prefetch=1,
        grid=(B,),
        in_specs=[pl.BlockSpec(memory_space=pl.ANY),
                  pl.BlockSpec((None, N_HEADS, HEAD_DIM), lambda b, t: (b, 0, 0)),
                  pl.BlockSpec((None, N_HEADS, HEAD_DIM), lambda b, t: (b, 0, 0)),
                  pl.BlockSpec((None, 1, LANES), lambda b, t: (b, 0, 0)),
                  pl.BlockSpec((None, 6 * KV_HEADS, HEAD_DIM), lambda b, t: (b, 0, 0)),
                  pl.BlockSpec((None, None, n_win, 2 * KV_HEADS, HEAD_DIM), lambda b, t: (layer, b, 0, 0, 0))],
        out_specs=pl.BlockSpec((None, N_HEADS, HEAD_DIM), lambda b, t: (b, 0, 0)),
        scratch_shapes=[pltpu.VMEM((KV_HEADS, n_keys, 2 * KV_HEADS, HEAD_DIM), f32),
                        pltpu.SemaphoreType.DMA((KV_HEADS,))],
    )
    return pl.pallas_call(
        kern,
        grid_spec=grid_spec,
        out_shape=jax.ShapeDtypeStruct((B, N_HEADS, HEAD_DIM), f32),
        compiler_params=_cparams(("arbitrary",)),
        name="nsa_decode",
    )(tok_off, cache_tok, q3, o_c, gn3, new3, state_win5)


def _rope_tables(pos):
    half = HEAD_DIM // 2
    inv = jnp.power(ROPE_THETA, -jnp.arange(0, HEAD_DIM, 2, dtype=f32) / HEAD_DIM)
    ang = pos.astype(f32)[:, None] * inv[None, :]
    cos, sin = jnp.cos(ang), jnp.sin(ang)
    return jnp.concatenate([cos, cos], axis=-1), jnp.concatenate([-sin, sin], axis=-1)


def _trunk(x, t0, past, p):
    B, T, D = x.shape
    M = B * T
    depth = p["w_in"].shape[0]
    d_pool = p["pool_scale"].shape[-1]
    d_main = d_pool + D_ATTN + 6 * D_KV
    gate_col = d_main
    mrg_col = d_main + N_HEADS * N_NSA_BRANCH
    tn_main = 512
    assert d_pool % tn_main == 0 and d_main % tn_main == 0 and gate_col % LANES == 0
    q_blk0 = d_pool // tn_main
    kv_blk0 = (d_pool + D_ATTN) // tn_main
    rope_blocks = tuple(range(q_blk0, kv_blk0)) + tuple(kv_blk0 + r for r in (0, 2, 4))
    kv_col = (d_pool + D_ATTN) // HEAD_DIM

    cos, sin = _rope_tables(t0 + jnp.arange(T))
    tm = _pick(M, 1024, 8)
    if T >= tm:
        n_tab = T // tm
    else:
        cos, sin = jnp.tile(cos, (tm // T, 1)), jnp.tile(sin, (tm // T, 1))
        n_tab = 1

    def rope_specs(tm_, tn_):
        assert tm_ == tm
        spec = pl.BlockSpec((tm, HEAD_DIM), lambda i, j, k: (i % n_tab, 0))
        return [spec, spec]

    x2 = x.reshape(M, D)
    w_mrg = p["w_in"][:, :, mrg_col:]
    wc = jnp.concatenate([p["cmp_w1"][:, :, :CMP_STRIDE], p["cmp_w1"][:, :, CMP_STRIDE:]], axis=-1)
    cb1 = p["cmp_b1"][:, :, None, :]
    if past is not None:
        cache_kv, page_table, state_win, state_pool = past
        n_pool = cache_kv.shape[1]
        n_pages = page_table.shape[1]
        past_len = n_pages * PAGE_SIZE
        assert T == 1 and t0 == past_len and state_win.shape[2] == WINDOW
        n_blk = past_len // SEL_LEN + 1
        assert n_blk >= N_SEL
        nbp = -(-n_blk // LANES) * LANES
        cache_pages = cache_kv.reshape(depth * n_pool, PAGE_SIZE, 4 * KV_HEADS, HEAD_DIM)
        cache_tok = cache_kv.reshape(depth * n_pool * PAGE_SIZE, 4 * KV_HEADS, HEAD_DIM)
        state_win5 = state_win.reshape(depth, B, WINDOW, 2 * KV_HEADS, HEAD_DIM)
        ov_s = jnp.repeat(_cmp_to_sel(past_len // CMP_STRIDE, n_blk, nbp), KV_HEADS, axis=0)

    precise = past is not None
    act = f32 if precise else bf16
    kv_new, win_new, pool_new = [], [], []
    for l in range(depth):
        xn = rmsnorm(x2, p["norm_mix"], l, act)
        proj = _matmul(xn, [p["w_in"]], (l,), d_main, 0, [cos, sin], rope_specs,
                       functools.partial(_ep_inproj, rope_blocks=rope_blocks), f32,
                       tm_pref=tm, tn_pref=tn_main, name="in_proj", precise=precise)
        gn = _matmul(xn, [p["w_in"]], (l,), LANES, gate_col // LANES, [], _no_extras, _ep_plain, f32,
                     tm_pref=tm, tn_pref=LANES, name="gate_proj", precise=precise)
        g_mrg = _matmul(xn, [w_mrg], (l,), 2 * D, 0, [], _no_extras, _ep_plain, f32, tm_pref=tm,
                        name="merge_gate_proj", precise=precise)

        rows = proj[:, d_pool + D_ATTN:d_pool + D_ATTN + 4 * D_KV].reshape(B, T, 4, KV_HEADS, HEAD_DIM)
        wrows = proj[:, d_pool + D_ATTN + 4 * D_KV:d_main].reshape(B, T, 2, KV_HEADS, HEAD_DIM)
        u = proj[:, :d_pool].reshape(B, T, d_pool)
        if past is None:
            proj3 = proj.reshape(B, T, d_main)
            pre16 = jnp.zeros((B, POOL_KEEP + 1, d_pool), f32)
            pool_out = pool_prompt(proj3, pre16, p["pool_w"], p["pool_scale"], l, t0).reshape(M, d_pool)
            kc = compress_prompt(proj3, wc, cb1, p["cmp_w2"], l, kv_col)
            attn = nsa_prompt(proj3, gn.reshape(B, T, LANES), kc, q_blk0 // (GROUP * HEAD_DIM // tn_main), kv_col)
            attn = attn.reshape(M, D_ATTN)
            win_new.append(wrows[:, -min(WINDOW, T):])
            pool_new.append(u[:, -POOL_KEEP:])
        else:
            state_t = jnp.transpose(state_pool[l], (1, 0, 2))
            pool_out = pool_sample(proj, state_t, p["pool_w"], p["pool_scale"], l, t0)
            hbuf = compress_pages(cache_pages, page_table, wc, l, n_pool)
            kc = compress_finish(hbuf, cb1, p["cmp_w2"], l)
            q3 = proj[:, d_pool:d_pool + D_ATTN].reshape(B, N_HEADS, HEAD_DIM)
            o_c, sel_idx = nsa_select(q3, kc, ov_s, t0, n_blk)
            n_gather = N_SEL - 1
            blk_idx = sel_idx[:, :, :n_gather]
            per_page = PAGE_SIZE // SEL_LEN
            page = jnp.take_along_axis(page_table, blk_idx.reshape(B, -1) // per_page, axis=1)
            tok_off = ((l * n_pool + page) * PAGE_SIZE + (blk_idx.reshape(B, -1) % per_page) * SEL_LEN)
            new3 = proj[:, d_pool + D_ATTN:d_main].reshape(B, 6 * KV_HEADS, HEAD_DIM)
            attn = nsa_decode(tok_off.reshape(-1).astype(i32), cache_tok, q3, o_c, gn.reshape(B, 1, LANES),
                              new3, state_win5, l, n_gather).reshape(M, D_ATTN)
            win_new.append(jnp.concatenate([state_win[l], wrows], axis=1)[:, -WINDOW:])
            pool_new.append(jnp.concatenate([state_pool[l], u], axis=1)[:, -POOL_KEEP:])
        kv_new.append(rows)

        pp = _matmul(pool_out, [p["w_pool_proj"]], (l,), D, 0, [], _no_extras, _ep_plain, f32, tm_pref=tm,
                     name="pool_proj", precise=precise)

        def merge_specs(tm_, tn_):
            nb = D // tn_
            return [pl.BlockSpec((tm_, tn_), lambda i, j, k: (i, j)),
                    pl.BlockSpec((tm_, tn_), lambda i, j, k: (i, j)),
                    pl.BlockSpec((tm_, tn_), lambda i, j, k: (i, j + nb))]

        h = _matmul(attn, [p["w_attn_proj"]], (l,), D, 0, [pp, g_mrg, g_mrg], merge_specs, _ep_merge, act,
                    tm_pref=tm, name="attn_proj_merge", precise=precise)

        def res_specs(tm_, tn_):
            return [pl.BlockSpec((tm_, tn_), lambda i, j, k: (i, j))]

        x2 = _matmul(h, [p["w_out"]], (l,), D, 0, [x2], res_specs, _ep_residual, f32, tm_pref=tm,
                     name="out_proj", precise=precise)

        xn = rmsnorm(x2, p["norm_ffn"], l, act if l % 2 == 0 else f32)
        if l % 2 == 0:
            hid = _matmul(xn, [p["ffn_wg"], p["ffn_wu"]], (l // 2,), p["ffn_wg"].shape[-1], 0, [], _no_extras,
                          _ep_swiglu, act, tm_pref=tm, name="ffn_up", precise=precise)
            x2 = _matmul(hid, [p["ffn_wd"]], (l // 2,), D, 0, [x2], res_specs, _ep_residual, f32,
                         tm_pref=tm, name="ffn_down", precise=precise)
        else:
            w_r = jnp.pad(p["moe_router"], ((0, 0), (0, 0), (0, LANES - N_EXPERTS)))
            comb = _matmul(xn, [w_r], (l // 2,), LANES, 0, [], _no_extras, _ep_router, f32, tm_pref=tm,
                           name="router", precise=precise)
            out = None
            if 2 * M >= N_EXPERTS * tm:
                gates, pos, src_tok, tile_expert, n_used = _route(comb, tm)
                xs = moe_dispatch(xn, src_tok, n_used, tm)
                hid = _matmul(xs, [p["moe_wg"], p["moe_wu"]], (l // 2,), p["moe_wg"].shape[-1], 0, [], _no_extras,
                              _ep_swiglu, bf16, tm_pref=tm, name="moe_up_sorted", group=(tile_expert, n_used))
                ys = _matmul(hid, [p["moe_wd"]], (l // 2,), D, 0, [], _no_extras, _ep_plain, f32,
                             tm_pref=tm, name="moe_down_sorted", group=(tile_expert, n_used))
                out = moe_combine(ys, pos, jnp.pad(gates, ((0, 0), (0, LANES - 2))), x2)
            for e in range(N_EXPERTS if out is None else 0):
                hid = _matmul(xn, [p["moe_wg"], p["moe_wu"]], (l // 2, e), p["moe_wg"].shape[-1], 0, [], _no_extras,
                              _ep_swiglu, bf16, tm_pref=tm, name="moe_up")
                first, last = e == 0, e == N_EXPERTS - 1
                extras = [comb] + ([] if first else [out]) + ([x2] if last else [])

                def moe_specs(tm_, tn_, n=len(extras)):
                    return ([pl.BlockSpec((tm_, LANES), lambda i, j, k: (i, 0))]
                            + [pl.BlockSpec((tm_, tn_), lambda i, j, k: (i, j))] * (n - 1))

                out = _matmul(hid, [p["moe_wd"]], (l // 2, e), D, 0, extras, moe_specs,
                              functools.partial(_ep_moe_acc, expert=e, first=first, last=last), f32,
                              tm_pref=tm, name="moe_down")
            x2 = out

    y = rmsnorm(x2, p["norm_final"][None], 0, f32).reshape(B, T, D)
    return y, jnp.stack(kv_new, 0), jnp.stack(win_new, 0), jnp.stack(pool_new, 0)


def kernel(x_prompt, x_sample, cache_kv, state_win, state_pool, page_table, norm_mix, w_in, pool_w, pool_scale,
           cmp_w1, cmp_b1, cmp_w2, w_pool_proj, w_attn_proj, w_out, norm_ffn, ffn_wg, ffn_wu, ffn_wd,
           moe_router, moe_wg, moe_wu, moe_wd, norm_final):
    p = dict(norm_mix=norm_mix, w_in=w_in, pool_w=pool_w, pool_scale=pool_scale, cmp_w1=cmp_w1, cmp_b1=cmp_b1,
             cmp_w2=cmp_w2, w_pool_proj=w_pool_proj, w_attn_proj=w_attn_proj, w_out=w_out, norm_ffn=norm_ffn,
             ffn_wg=ffn_wg, ffn_wu=ffn_wu, ffn_wd=ffn_wd, moe_router=moe_router, moe_wg=moe_wg, moe_wu=moe_wu,
             moe_wd=moe_wd, norm_final=norm_final)
    y_p, kv_p, win_p, pool_p = _trunk(x_prompt, 0, None, p)
    past_len = page_table.shape[1] * PAGE_SIZE
    y_s, kv_s, win_s, pool_s = _trunk(x_sample, past_len, (cache_kv, page_table, state_win, state_pool), p)
    return (y_p, y_s, kv_p, kv_s, win_p, win_s, pool_p, pool_s)
```
